```python
import jax, jax.numpy as jnp
from jax import lax
import numpy as np

D_MODEL = 1024
BATCH = 2
SEQ = 8192
DEPTH = 4

CHUNK = 64
EPS = 1e-6
ROPE_THETA = 10000.0

A_HEADS = 4
A_DK = 128
A_DV = 128
A_WIDTH = A_HEADS * A_DV
B_GROUPS = 4
B_GROUP_DIM = 128
B_WIDTH = B_GROUPS * B_GROUP_DIM
B_BLOCK = 128
C_HEADS = 4
C_HEAD_DIM = 128
C_WIDTH = C_HEADS * C_HEAD_DIM
IDX_HEADS = 8
IDX_DIM = 64
DSA_TOPK_MAX = 256
Q_BLOCK = 128
N_BRANCH = 3
BRANCH_WIDTH = 512
PEER_HEADS = 8
PEER_N_KEYS = 128
PEER_N_EXPERTS = PEER_N_KEYS * PEER_N_KEYS
PEER_KEY_DIM = 128
PEER_TOPK = 16
PEER_BLOCK = 128

IN_SIZES = (A_HEADS * A_DK, A_HEADS * A_DK, A_WIDTH, A_WIDTH,
            B_WIDTH, B_WIDTH,
            C_WIDTH, C_WIDTH, C_WIDTH,
            IDX_HEADS * IDX_DIM, IDX_DIM, IDX_HEADS,
            N_BRANCH * D_MODEL)
N_IN = sum(IN_SIZES)

kernel_name = "hybrid_hgrn2_gmlp_dsa_peer_trunk"


def rmsnorm(x, gain):
    xf = x.astype(jnp.float32)
    y = xf * lax.rsqrt(jnp.mean(xf * xf, axis=-1, keepdims=True) + EPS)
    return (y * gain.astype(jnp.float32)).astype(x.dtype)


def rope(x, positions):
    half = x.shape[-1] // 2
    inv = ROPE_THETA ** (-jnp.arange(half, dtype=jnp.float32) / half)
    ang = positions.astype(jnp.float32)[..., None] * inv
    cos = jnp.cos(ang)[:, :, None, :]
    sin = jnp.sin(ang)[:, :, None, :]
    x1 = x[..., :half].astype(jnp.float32)
    x2 = x[..., half:].astype(jnp.float32)
    out = jnp.concatenate([x1 * cos - x2 * sin, x2 * cos + x1 * sin], axis=-1)
    return out.astype(x.dtype)


def hgrn2_mixer(q, f_logit, i, g, lb, norm_gain):
    B, S, _ = q.shape
    nc = S // CHUNK
    f32 = jnp.float32
    lbf = lb.astype(f32)
    z = f_logit.astype(f32)
    logf = jnp.logaddexp(jnp.log(lbf), jnp.log1p(-lbf) + jax.nn.log_sigmoid(z))
    key = (1.0 - lbf) * jax.nn.sigmoid(-z)
    qf = jax.nn.silu(q.astype(f32)) * (A_DK ** -0.5)
    vf = i.astype(f32)

    def to_chunks(t, d):
        return t.reshape(B, nc, CHUNK, A_HEADS, d).transpose(1, 0, 3, 2, 4)

    tri = jnp.tril(jnp.ones((CHUNK, CHUNK), dtype=bool))

    def step(state, inp):
        qc, kc, vc, lfc = inp
        b = jnp.cumsum(lfc, axis=2)
        diff = b[:, :, :, None, :] - b[:, :, None, :, :]
        decay = jnp.exp(jnp.where(tri[:, :, None], diff, -jnp.inf))
        attn = jnp.einsum('bhtd,bhtsd,bhsd->bhts', qc, decay, kc)
        o = attn @ vc + jnp.einsum('bhtd,bhde->bhte', qc * jnp.exp(b), state)
        b_last = b[:, :, -1:, :]
        state = (jnp.exp(b_last[:, :, 0, :, None]) * state
                 + jnp.einsum('bhsd,bhse->bhde', kc * jnp.exp(b_last - b), vc))
        return state, o

    state0 = jnp.zeros((B, A_HEADS, A_DK, A_DV), f32)
    _, o = lax.scan(step, state0, (to_chunks(qf, A_DK), to_chunks(key, A_DK),
                                   to_chunks(vf, A_DV), to_chunks(logf, A_DK)))
    o = o.transpose(1, 0, 3, 2, 4).reshape(B, S, A_HEADS, A_DV)
    o = rmsnorm(o, norm_gain)
    gate = jax.nn.silu(g.astype(f32)).reshape(B, S, A_HEADS, A_DV)
    return (o * gate).reshape(B, S, A_WIDTH).astype(g.dtype)


def gmlp_mixer(u, v, norm_gain, w_s, b_s):
    B, S, _ = u.shape
    u = jax.nn.gelu(u)
    v = rmsnorm(jax.nn.gelu(v), norm_gain)
    nb = S // B_BLOCK
    vb = v.reshape(B, nb, B_BLOCK, B_GROUPS, B_GROUP_DIM)
    w = w_s * jnp.tril(jnp.ones((B_BLOCK, B_BLOCK), w_s.dtype))
    mixed = jnp.einsum('gts,bnsgc->bntgc', w, vb) + b_s.T[None, None, :, :, None]
    return u * mixed.reshape(B, S, B_WIDTH)


def dsa_mixer(q, k, v, qi, ki, wi, positions):
    B, S, _ = q.shape
    f32 = jnp.float32
    topk = min(DSA_TOPK_MAX, S // 4)
    q = rope(q.reshape(B, S, C_HEADS, C_HEAD_DIM), positions)
    k = rope(k.reshape(B, S, C_HEADS, C_HEAD_DIM), positions)
    v = v.reshape(B, S, C_HEADS, C_HEAD_DIM)
    qi = rope(qi.reshape(B, S, IDX_HEADS, IDX_DIM), positions)
    ki = rope(ki.reshape(B, S, 1, IDX_DIM), positions)[:, :, 0].astype(f32)
    wi = wi.astype(f32) * (IDX_HEADS ** -0.5)
    nq = S // Q_BLOCK
    key_chunk = jnp.arange(S) // CHUNK
    b_idx = jnp.arange(B)[:, None, None]

    def blocks(t):
        return t.reshape(B, nq, Q_BLOCK, *t.shape[2:]).swapaxes(0, 1)

    def attend(inp):
        q_blk, qi_blk, wi_blk, blk = inp
        q_chunk = (blk * Q_BLOCK + jnp.arange(Q_BLOCK)) // CHUNK
        admissible = key_chunk[None, :] <= q_chunk[:, None]
        logits = jnp.einsum('bthd,bsd->bths', qi_blk.astype(f32), ki) * (IDX_DIM ** -0.5)
        score = jnp.einsum('bth,bths->bts', wi_blk, jax.nn.relu(logits))
        score = jnp.where(admissible[None], score, -jnp.inf)
        _, sel = lax.top_k(score, topk)
        valid = key_chunk[sel] <= q_chunk[None, :, None]
        k_sel = k[b_idx, sel]
        v_sel = v[b_idx, sel]
        s = jnp.einsum('bthd,btkhd->bthk', q_blk.astype(f32), k_sel.astype(f32)) * (C_HEAD_DIM ** -0.5)
        s = jnp.where(valid[:, :, None, :], s, -jnp.inf)
        p = jax.nn.softmax(s, axis=-1)
        return jnp.einsum('bthk,btkhd->bthd', p.astype(v.dtype), v_sel)

    out = lax.map(attend, (blocks(q), blocks(qi), blocks(wi), jnp.arange(nq)))
    return out.swapaxes(0, 1).reshape(B, S, C_WIDTH)


def peer_ffn(h, w_q, sub_keys, u_tab, v_tab):
    B, S, _ = h.shape
    f32 = jnp.float32
    qry = (h @ w_q).reshape(B, S, PEER_HEADS, 2, PEER_KEY_DIM)
    sc = jnp.einsum('bshpd,hpnd->bshpn', qry.astype(f32), sub_keys.astype(f32))
    top_v, top_i = lax.top_k(sc, PEER_TOPK)
    cand = (top_v[..., 0, :, None] + top_v[..., 1, None, :]).reshape(B, S, PEER_HEADS, PEER_TOPK * PEER_TOPK)
    cand_id = (top_i[..., 0, :, None] * PEER_N_KEYS + top_i[..., 1, None, :]).reshape(B, S, PEER_HEADS, PEER_TOPK * PEER_TOPK)
    best_v, best_pos = lax.top_k(cand, PEER_TOPK)
    expert = jnp.take_along_axis(cand_id, best_pos, axis=-1)
    gate = jax.nn.softmax(best_v, axis=-1)
    nb = S // PEER_BLOCK

    def blocks(t):
        return t.reshape(B, nb, PEER_BLOCK, *t.shape[2:]).swapaxes(0, 1)

    def experts(inp):
        h_blk, e_blk, g_blk = inp
        act = jax.nn.gelu(jnp.einsum('btd,bthkd->bthk', h_blk, u_tab[e_blk]))
        return jnp.einsum('bthk,bthkd->btd', (g_blk * act).astype(h.dtype), v_tab[e_blk])

    out = lax.map(experts, (blocks(h), blocks(expert), blocks(gate)))
    return out.swapaxes(0, 1).reshape(B, S, D_MODEL)


def setup_inputs(seed: int = 0) -> dict:
    key = jax.random.key(seed)
    ks = jax.random.split(key, 17)
    f32 = jnp.float32
    nrm = lambda k, shape, s: jax.random.normal(k, shape, f32) * s
    x = jax.random.normal(ks[0], (BATCH, SEQ, D_MODEL), f32)
    offset = jax.random.randint(ks[1], (BATCH, 1), 0, 4096, dtype=jnp.int32)
    positions = (offset + jnp.arange(SEQ, dtype=jnp.int32)[None, :]).astype(jnp.int32)
    return {
        "x": x,
        "positions": positions,
        "norm1_gain": 1.0 + nrm(ks[2], (DEPTH, D_MODEL), 0.05),
        "w_in": nrm(ks[3], (DEPTH, D_MODEL, N_IN), D_MODEL ** -0.5),
        "hgrn_lb_logits": nrm(ks[4], (DEPTH, A_HEADS * A_DK), 0.5),
        "hgrn_norm_gain": 1.0 + nrm(ks[5], (DEPTH, A_DV), 0.05),
        "gmlp_norm_gain": 1.0 + nrm(ks[6], (DEPTH, B_WIDTH), 0.05),
        "gmlp_w_s": nrm(ks[7], (DEPTH, B_GROUPS, B_BLOCK, B_BLOCK), B_BLOCK ** -0.5),
        "gmlp_b_s": nrm(ks[8], (DEPTH, B_GROUPS, B_BLOCK), 0.02),
        "w_branch": nrm(ks[9], (DEPTH, N_BRANCH, BRANCH_WIDTH, D_MODEL), BRANCH_WIDTH ** -0.5),
        "w_out": nrm(ks[10], (DEPTH, D_MODEL, D_MODEL), D_MODEL ** -0.5),
        "norm2_gain": 1.0 + nrm(ks[11], (DEPTH, D_MODEL), 0.05),
        "peer_w_q": nrm(ks[12], (DEPTH, D_MODEL, PEER_HEADS * 2 * PEER_KEY_DIM), D_MODEL ** -0.5),
        "peer_sub_keys": nrm(ks[13], (DEPTH, PEER_HEADS, 2, PEER_N_KEYS, PEER_KEY_DIM), PEER_KEY_DIM ** -0.5),
        "peer_u": nrm(ks[14], (DEPTH, PEER_N_EXPERTS, D_MODEL), D_MODEL ** -0.5),
        "peer_v": nrm(ks[15], (DEPTH, PEER_N_EXPERTS, D_MODEL), (PEER_HEADS * PEER_TOPK) ** -0.5),
        "final_gain": 1.0 + nrm(ks[16], (D_MODEL,), 0.05),
    }


def reference(x, positions, norm1_gain, w_in, hgrn_lb_logits, hgrn_norm_gain,
              gmlp_norm_gain, gmlp_w_s, gmlp_b_s, w_branch, w_out, norm2_gain,
              peer_w_q, peer_sub_keys, peer_u, peer_v, final_gain):
    B, S, _ = x.shape
    split_at = np.cumsum(IN_SIZES)[:-1].tolist()
    p = jax.nn.softmax(hgrn_lb_logits.astype(jnp.float32), axis=0)
    c = jnp.cumsum(p, axis=0)
    lb_all = c - c[0:1]
    for l in range(DEPTH):
        h = rmsnorm(x, norm1_gain[l])
        proj = h @ w_in[l]
        (aq, af, ai, ag, bu, bv, cq, ck, cv, iq, ik, iw, gl) = jnp.split(proj, split_at, axis=-1)
        y_a = hgrn2_mixer(aq, af, ai, ag, lb_all[l], hgrn_norm_gain[l])
        y_b = gmlp_mixer(bu, bv, gmlp_norm_gain[l], gmlp_w_s[l], gmlp_b_s[l])
        y_c = dsa_mixer(cq, ck, cv, iq, ik, iw, positions)
        branches = jnp.stack([y_a, y_b.astype(y_a.dtype), y_c.astype(y_a.dtype)], axis=2)
        branch_proj = jnp.einsum('bsnw,nwd->bsnd', branches, w_branch[l])
        gates = jax.nn.sigmoid(gl.reshape(B, S, N_BRANCH, D_MODEL))
        x = x + jnp.sum(gates * branch_proj, axis=2) @ w_out[l]
        x = x + peer_ffn(rmsnorm(x, norm2_gain[l]), peer_w_q[l], peer_sub_keys[l], peer_u[l], peer_v[l])
    return rmsnorm(x, final_gain)
```

```python
import functools

import numpy as np
import jax
import jax.numpy as jnp
from jax import lax
from jax.experimental import pallas as pl
from jax.experimental.pallas import tpu as pltpu

F32 = jnp.float32
BF16 = jnp.bfloat16

D_MODEL = 1024
CHUNK = 64
EPS = 1e-6
ROPE_THETA = 10000.0
A_HEADS = 4
A_DK = 128
B_GROUPS = 4
B_BLOCK = 128
C_HEADS = 4
C_HEAD_DIM = 128
IDX_HEADS = 8
IDX_DIM = 64
DSA_TOPK_MAX = 256
Q_BLOCK = 128
N_BRANCH = 3
BRANCH_WIDTH = 512
PEER_HEADS = 8
PEER_N_KEYS = 128
PEER_TOPK = 16

LANES = 128
VMEM_LIMIT = 56 * 1024 * 1024

N_MAIN = 10 * 512
N_PACKED = N_BRANCH * D_MODEL + N_MAIN + LANES
COLBLK_MAIN = (N_BRANCH * D_MODEL) // 512
COLBLK_IDX = (N_BRANCH * D_MODEL + N_MAIN) // LANES

NEG_BIG = -1e30
INT_MIN = -2147483648


def _cparams(sem, vmem=None):
    return pltpu.CompilerParams(dimension_semantics=sem, vmem_limit_bytes=vmem)


def _rms(x, gain):
    return x * lax.rsqrt(jnp.mean(x * x, axis=-1, keepdims=True) + EPS) * gain


def _dot_t(a, b):
    return lax.dot_general(a, b, (((1,), (1,)), ((), ())), preferred_element_type=F32)


def _inproj_kernel(x_ref, g_ref, w_ref, o_ref, h_scr):
    @pl.when(pl.program_id(1) == 0)
    def _():
        h_scr[...] = _rms(x_ref[...], g_ref[...]).astype(BF16)

    o_ref[...] = jnp.dot(h_scr[...], w_ref[...], preferred_element_type=F32)


def _inproj(x, gain, w_packed, tm=1024, tn=640):
    T = x.shape[0]
    tm = min(tm, T)
    return pl.pallas_call(
        _inproj_kernel,
        grid=(T // tm, N_PACKED // tn),
        in_specs=[pl.BlockSpec((tm, D_MODEL), lambda i, j: (i, 0)),
                  pl.BlockSpec((1, D_MODEL), lambda i, j: (0, 0)),
                  pl.BlockSpec((D_MODEL, tn), lambda i, j: (0, j))],
        out_specs=pl.BlockSpec((tm, tn), lambda i, j: (i, j)),
        out_shape=jax.ShapeDtypeStruct((T, N_PACKED), F32),
        scratch_shapes=[pltpu.VMEM((tm, D_MODEL), BF16)],
        compiler_params=_cparams(("parallel", "arbitrary"), VMEM_LIMIT),
        name="inproj",
    )(x, gain, w_packed)


HG_C = 128
HG_SUB = 16


def _hgrn_kernel(q_ref, f_ref, i_ref, g_ref, lbp_ref, gain_ref, o_ref, st_scr):
    @pl.when(pl.program_id(1) == 0)
    def _():
        st_scr[...] = jnp.zeros_like(st_scr)

    C = HG_C
    row = lax.broadcasted_iota(jnp.int32, (C, 1), 0)
    rowm = lax.broadcasted_iota(jnp.int32, (C, C), 0)
    colm = lax.broadcasted_iota(jnp.int32, (C, C), 1)
    gain = gain_ref[...]

    for h in range(A_HEADS):
        sl = slice(h * A_DK, (h + 1) * A_DK)
        z = f_ref[:, sl]
        loglb = lbp_ref[0:1, sl]
        log1mlb = lbp_ref[1:2, sl]
        omlb = lbp_ref[2:3, sl]
        lsig = -(jnp.maximum(-z, 0.0) + jnp.log1p(jnp.exp(-jnp.abs(z))))
        cterm = log1mlb + lsig
        logf = jnp.maximum(loglb, cterm) + jnp.log1p(jnp.exp(-jnp.abs(loglb - cterm)))
        key = omlb * jax.nn.sigmoid(-z)
        qf = jax.nn.silu(q_ref[:, sl]) * (A_DK ** -0.5)
        vf = i_ref[:, sl]

        b = logf
        sh = 1
        while sh < C:
            b = b + jnp.where(row >= sh, pltpu.roll(b, sh, 0), 0.0)
            sh *= 2

        attn = jnp.zeros((C, C), F32)
        m = C // 2
        while m >= HG_SUB:
            ref = jnp.concatenate(
                [jnp.broadcast_to(b[j * 2 * m + m - 1:j * 2 * m + m, :], (2 * m, A_DK))
                 for j in range(C // (2 * m))], axis=0)
            qt = qf * jnp.exp(jnp.minimum(b - ref, 0.0))
            kt = key * jnp.exp(jnp.minimum(ref - b, 0.0))
            a = _dot_t(qt.astype(BF16), kt.astype(BF16))
            msk = ((rowm // (2 * m)) == (colm // (2 * m))) & ((rowm % (2 * m)) >= m) & ((colm % (2 * m)) < m)
            attn = attn + jnp.where(msk, a, 0.0)
            m //= 2

        nsub = C // HG_SUB
        for j in range(HG_SUB):
            kj = jnp.concatenate(
                [jnp.broadcast_to(key[i * HG_SUB + j:i * HG_SUB + j + 1, :], (HG_SUB, A_DK))
                 for i in range(nsub)], axis=0)
            bj = jnp.concatenate(
                [jnp.broadcast_to(b[i * HG_SUB + j:i * HG_SUB + j + 1, :], (HG_SUB, A_DK))
                 for i in range(nsub)], axis=0)
            d = jnp.exp(jnp.minimum(b - bj, 0.0)) * qf * kj
            red = jnp.sum(d, axis=-1, keepdims=True)
            hit = (colm == (rowm // HG_SUB) * HG_SUB + j) & ((rowm % HG_SUB) >= j)
            attn = jnp.where(hit, red, attn)

        st = st_scr[h]
        o = jnp.dot(attn.astype(BF16), vf.astype(BF16), preferred_element_type=F32)
        o = o + _dot_t((qf * jnp.exp(b)).astype(BF16), st.astype(BF16))
        blast = b[C - 1:C, :]
        kdec = key * jnp.exp(blast - b)
        upd = lax.dot_general(vf.astype(BF16), kdec.astype(BF16), (((0,), (0,)), ((), ())),
                              preferred_element_type=F32)
        st_scr[h] = st * jnp.exp(blast) + upd

        o = _rms(o, gain)
        o_ref[:, sl] = o * jax.nn.silu(g_ref[:, sl])


def _hgrn(proj, lbp, gain, B, S):
    nc = S // HG_C
    cb = COLBLK_MAIN

    def spec(k):
        return pl.BlockSpec((HG_C, 512), lambda b, c, k=k: (b * nc + c, cb + k))

    return pl.pallas_call(
        _hgrn_kernel,
        grid=(B, nc),
        in_specs=[spec(0), spec(1), spec(2), spec(3),
                  pl.BlockSpec((3, 512), lambda b, c: (0, 0)),
                  pl.BlockSpec((1, A_DK), lambda b, c: (0, 0))],
        out_specs=pl.BlockSpec((HG_C, 512), lambda b, c: (b * nc + c, 0)),
        out_shape=jax.ShapeDtypeStruct((B * S, 512), F32),
        scratch_shapes=[pltpu.VMEM((A_HEADS, A_DK, A_DK), F32)],
        compiler_params=_cparams(("parallel", "arbitrary")),
        name="hgrn",
    )(proj, proj, proj, proj, lbp, gain)


def _gmlp_kernel(u_ref, v_ref, g_ref, w_ref, b_ref, o_ref, *, nsub):
    r = lax.broadcasted_iota(jnp.int32, (B_BLOCK, B_BLOCK), 0)
    c = lax.broadcasted_iota(jnp.int32, (B_BLOCK, B_BLOCK), 1)
    tril = r >= c
    u = jax.nn.gelu(u_ref[...])
    v = _rms(jax.nn.gelu(v_ref[...]), g_ref[...]).astype(BF16)
    bias = b_ref[...]
    for g in range(B_GROUPS):
        w = jnp.where(tril, w_ref[g], 0.0).astype(BF16)
        cs = slice(g * LANES, (g + 1) * LANES)
        for s in range(nsub):
            rs = slice(s * B_BLOCK, (s + 1) * B_BLOCK)
            mixed = jnp.dot(w, v[rs, cs], preferred_element_type=F32) + bias[:, cs]
            o_ref[rs, cs] = u[rs, cs] * mixed


def _gmlp(proj, gain, w_s, bias_full, tb=512):
    T = proj.shape[0]
    tb = min(tb, T)
    cb = COLBLK_MAIN
    return pl.pallas_call(
        functools.partial(_gmlp_kernel, nsub=tb // B_BLOCK),
        grid=(T // tb,),
        in_specs=[pl.BlockSpec((tb, 512), lambda i: (i, cb + 4)),
                  pl.BlockSpec((tb, 512), lambda i: (i, cb + 5)),
                  pl.BlockSpec((1, 512), lambda i: (0, 0)),
                  pl.BlockSpec((B_GROUPS, B_BLOCK, B_BLOCK), lambda i: (0, 0, 0)),
                  pl.BlockSpec((B_BLOCK, 512), lambda i: (0, 0))],
        out_specs=pl.BlockSpec((tb, 512), lambda i: (i, 0)),
        out_shape=jax.ShapeDtypeStruct((T, 512), F32),
        compiler_params=_cparams(("parallel",)),
        name="gmlp",
    )(proj, proj, gain, w_s, bias_full)


def _dsa_prep_kernel(q_ref, k_ref, v_ref, iq_ref, idx_ref, cc_ref, sc_ref, ci_ref, si_ref,
                     qo_ref, ko_ref, vo_ref, qio_ref, kio_ref):
    cc = cc_ref[...]
    sc = sc_ref[...]
    ci = ci_ref[...]
    si = si_ref[...]
    lane = lax.broadcasted_iota(jnp.int32, ci.shape, 1)
    first_half = (lane % IDX_DIM) < (IDX_DIM // 2)

    def rope_i(x):
        partner = jnp.where(first_half, pltpu.roll(x, LANES - IDX_DIM // 2, 1),
                            pltpu.roll(x, IDX_DIM // 2, 1))
        return x * ci + partner * si

    for h in range(C_HEADS):
        sl = slice(h * LANES, (h + 1) * LANES)
        x = q_ref[:, sl]
        qo_ref[:, sl] = ((x * cc + pltpu.roll(x, C_HEAD_DIM // 2, 1) * sc)
                         * (C_HEAD_DIM ** -0.5)).astype(BF16)
        x = k_ref[:, sl]
        ko_ref[:, sl] = (x * cc + pltpu.roll(x, C_HEAD_DIM // 2, 1) * sc).astype(BF16)
    vo_ref[...] = v_ref[...].astype(BF16)
    for j in range(IDX_HEADS // 2):
        r = rope_i(iq_ref[:, j * LANES:(j + 1) * LANES])
        qio_ref[:, (2 * j) * LANES:(2 * j + 1) * LANES] = r.astype(BF16)
        qio_ref[:, (2 * j + 1) * LANES:(2 * j + 2) * LANES] = pltpu.roll(r, IDX_DIM, 1).astype(BF16)
    rk = rope_i(idx_ref[...])
    kio_ref[...] = jnp.where(lane < IDX_DIM, rk, 0.0).astype(BF16)


def _dsa_prep(proj, cos_c, sin_c, cos_i, sin_i, tb=512):
    T = proj.shape[0]
    tb = min(tb, T)
    cb = COLBLK_MAIN

    def pspec(k):
        return pl.BlockSpec((tb, 512), lambda i, k=k: (i, cb + k))

    tspec = pl.BlockSpec((tb, LANES), lambda i: (i, 0))
    return pl.pallas_call(
        _dsa_prep_kernel,
        grid=(T // tb,),
        in_specs=[pspec(6), pspec(7), pspec(8), pspec(9),
                  pl.BlockSpec((tb, LANES), lambda i: (i, COLBLK_IDX)),
                  tspec, tspec, tspec, tspec],
        out_specs=[pl.BlockSpec((tb, 512), lambda i: (i, 0)),
                   pl.BlockSpec((tb, 512), lambda i: (i, 0)),
                   pl.BlockSpec((tb, 512), lambda i: (i, 0)),
                   pl.BlockSpec((tb, IDX_HEADS * LANES), lambda i: (i, 0)),
                   pl.BlockSpec((tb, LANES), lambda i: (i, 0))],
        out_shape=[jax.ShapeDtypeStruct((T, 512), BF16),
                   jax.ShapeDtypeStruct((T, 512), BF16),
                   jax.ShapeDtypeStruct((T, 512), BF16),
                   jax.ShapeDtypeStruct((T, IDX_HEADS * LANES), BF16),
                   jax.ShapeDtypeStruct((T, LANES), BF16)],
        compiler_params=_cparams(("parallel",)),
        name="dsa_prep",
    )(proj, proj, proj, proj, proj, cos_c, sin_c, cos_i, sin_i)


DSA_TK = 512


def _sortable(score):
    score = jnp.where(score == 0.0, 0.0, score)
    bits = pltpu.bitcast(score, jnp.int32)
    return bits ^ ((bits >> 31) & 0x7FFFFFFF)


def _dsa_kernel(q_ref, qi_ref, idx_ref, k_ref, v_ref, ki_ref, tri_ref, o_ref,
                qidx_scr, wb_scr, key_scr, *, topk):
    TK = DSA_TK
    qb = pl.program_id(1)
    nkb = (qb * Q_BLOCK + Q_BLOCK + TK - 1) // TK
    wscale = (IDX_HEADS ** -0.5) * (IDX_DIM ** -0.5)

    iw = idx_ref[...]
    for h in range(IDX_HEADS):
        qidx_scr[h * Q_BLOCK:(h + 1) * Q_BLOCK, :] = qi_ref[:, h * LANES:(h + 1) * LANES]
        wb_scr[h * Q_BLOCK:(h + 1) * Q_BLOCK, :] = jnp.broadcast_to(
            iw[:, IDX_DIM + h:IDX_DIM + h + 1] * wscale, (Q_BLOCK, LANES))

    qchunk = (qb * Q_BLOCK + lax.broadcasted_iota(jnp.int32, (Q_BLOCK, 1), 0)) // CHUNK
    lane_tk = lax.broadcasted_iota(jnp.int32, (Q_BLOCK, TK), 1)

    def p1(kb, carry):
        ki = ki_ref[pl.ds(pl.multiple_of(kb * TK, TK), TK), :]
        logits = _dot_t(qidx_scr[...], ki)
        tiles = []
        for c in range(TK // LANES):
            s = jnp.zeros((Q_BLOCK, LANES), F32)
            for h in range(IDX_HEADS):
                lg = logits[h * Q_BLOCK:(h + 1) * Q_BLOCK, c * LANES:(c + 1) * LANES]
                s = s + jnp.maximum(lg, 0.0) * wb_scr[h * Q_BLOCK:(h + 1) * Q_BLOCK, :]
            tiles.append(s)
        score = jnp.concatenate(tiles, axis=1)
        adm = ((kb * TK + lane_tk) // CHUNK) <= qchunk
        score = jnp.where(adm, score, -jnp.inf)
        key_scr[kb] = _sortable(score)
        return carry

    lax.fori_loop(0, nkb, p1, 0)

    def count_ge(cand):
        def body(kb, acc):
            kt = key_scr[kb]
            for c in range(TK // LANES):
                acc = acc + jnp.where(kt[:, c * LANES:(c + 1) * LANES] >= cand, 1, 0)
            return acc
        acc = lax.fori_loop(0, nkb, body, jnp.zeros((Q_BLOCK, LANES), jnp.int32))
        return jnp.sum(acc, axis=-1, keepdims=True)

    zero = jnp.zeros((Q_BLOCK, 1), jnp.int32)
    base = jnp.where(count_ge(zero) >= topk, zero, zero + INT_MIN)

    def bis(i, base):
        cand = base | (jnp.int32(1) << (30 - i))
        return jnp.where(count_ge(cand) >= topk, cand, base)

    tau = lax.fori_loop(0, 31, bis, base)
    need = (topk - count_ge(tau + 1)).astype(F32)

    key_neg_inf = _sortable(jnp.full((1, 1), -jnp.inf, F32))

    def p2(kb, carry):
        eq_seen, ms, ls, accs = carry
        off = pl.multiple_of(kb * TK, TK)
        kt = key_scr[kb]
        eq = kt == tau
        prefix = jnp.dot(jnp.where(eq, 1.0, 0.0).astype(BF16), tri_ref[...],
                         preferred_element_type=F32)
        sel = (kt > tau) | (eq & ((eq_seen + prefix) <= need))
        sel = sel & (kt > key_neg_inf)
        eq_seen = eq_seen + prefix[:, TK - 1:TK]
        new_ms, new_ls, new_accs = [], [], []
        for h in range(C_HEADS):
            sl = slice(h * LANES, (h + 1) * LANES)
            s = _dot_t(q_ref[:, sl], k_ref[pl.ds(off, TK), sl])
            s = jnp.where(sel, s, NEG_BIG)
            m_new = jnp.maximum(ms[h], jnp.max(s, axis=-1, keepdims=True))
            alpha = jnp.exp(ms[h] - m_new)
            p = jnp.where(sel, jnp.exp(s - m_new), 0.0)
            new_ls.append(alpha * ls[h] + jnp.sum(p, axis=-1, keepdims=True))
            new_accs.append(alpha * accs[h] + jnp.dot(p.astype(BF16), v_ref[pl.ds(off, TK), sl],
                                                       preferred_element_type=F32))
            new_ms.append(m_new)
        return eq_seen, tuple(new_ms), tuple(new_ls), tuple(new_accs)

    col0 = jnp.zeros((Q_BLOCK, 1), F32)
    init = (col0,
            tuple(col0 + NEG_BIG for _ in range(C_HEADS)),
            tuple(col0 for _ in range(C_HEADS)),
            tuple(jnp.zeros((Q_BLOCK, LANES), F32) for _ in range(C_HEADS)))
    _, ms, ls, accs = lax.fori_loop(0, nkb, p2, init)
    for h in range(C_HEADS):
        o_ref[:, h * LANES:(h + 1) * LANES] = accs[h] / ls[h]


def _dsa(q_r, qi_r, proj, k_r, v_b, ki_r, tri, B, S):
    nq = S // Q_BLOCK
    topk = min(DSA_TOPK_MAX, S // 4)
    nkb_max = (S + DSA_TK - 1) // DSA_TK
    Sp = nkb_max * DSA_TK
    assert Sp == S, "sequence length must be a multiple of the key tile"
    return pl.pallas_call(
        functools.partial(_dsa_kernel, topk=topk),
        grid=(B, nq),
        in_specs=[pl.BlockSpec((Q_BLOCK, 512), lambda b, q: (b * nq + q, 0)),
                  pl.BlockSpec((Q_BLOCK, IDX_HEADS * LANES), lambda b, q: (b * nq + q, 0)),
                  pl.BlockSpec((Q_BLOCK, LANES), lambda b, q: (b * nq + q, COLBLK_IDX)),
                  pl.BlockSpec((S, 512), lambda b, q: (b, 0)),
                  pl.BlockSpec((S, 512), lambda b, q: (b, 0)),
                  pl.BlockSpec((S, LANES), lambda b, q: (b, 0)),
                  pl.BlockSpec((DSA_TK, DSA_TK), lambda b, q: (0, 0))],
        out_specs=pl.BlockSpec((Q_BLOCK, 512), lambda b, q: (b * nq + q, 0)),
        out_shape=jax.ShapeDtypeStruct((B * S, 512), F32),
        scratch_shapes=[pltpu.VMEM((IDX_HEADS * Q_BLOCK, LANES), BF16),
                        pltpu.VMEM((IDX_HEADS * Q_BLOCK, LANES), F32),
                        pltpu.VMEM((nkb_max, Q_BLOCK, DSA_TK), jnp.int32)],
        compiler_params=_cparams(("parallel", "arbitrary"), VMEM_LIMIT),
        name="dsa",
    )(q_r, qi_r, proj, k_r, v_b, ki_r, tri)


def _merge_kernel(ya_ref, yb_ref, yc_ref, g0_ref, g1_ref, g2_ref, wb_ref, wo_ref, x_ref, o_ref):
    mixed = None
    for y_ref, g_ref, n in ((ya_ref, g0_ref, 0), (yb_ref, g1_ref, 1), (yc_ref, g2_ref, 2)):
        bp = jnp.dot(y_ref[...].astype(BF16), wb_ref[n], preferred_element_type=F32)
        t = jax.nn.sigmoid(g_ref[...]) * bp
        mixed = t if mixed is None else mixed + t
    o_ref[...] = x_ref[...] + jnp.dot(mixed.astype(BF16), wo_ref[...], preferred_element_type=F32)


def _merge(y_a, y_b, y_c, proj, w_branch, w_out, x, tb=256):
    T = x.shape[0]
    tb = min(tb, T)
    yspec = pl.BlockSpec((tb, 512), lambda i: (i, 0))

    def gspec(n):
        return pl.BlockSpec((tb, D_MODEL), lambda i, n=n: (i, n))

    return pl.pallas_call(
        _merge_kernel,
        grid=(T // tb,),
        in_specs=[yspec, yspec, yspec, gspec(0), gspec(1), gspec(2),
                  pl.BlockSpec((N_BRANCH, BRANCH_WIDTH, D_MODEL), lambda i: (0, 0, 0)),
                  pl.BlockSpec((D_MODEL, D_MODEL), lambda i: (0, 0)),
                  pl.BlockSpec((tb, D_MODEL), lambda i: (i, 0))],
        out_specs=pl.BlockSpec((tb, D_MODEL), lambda i: (i, 0)),
        out_shape=jax.ShapeDtypeStruct((T, D_MODEL), F32),
        compiler_params=_cparams(("parallel",), VMEM_LIMIT),
        name="merge",
    )(y_a, y_b, y_c, proj, proj, proj, w_branch, w_out, x)


def _extract16(x, lane):
    vals = []
    rank = jnp.full(x.shape, float(PEER_N_KEYS - 1), F32)
    for k in range(PEER_TOPK):
        m = jnp.max(x, axis=-1, keepdims=True)
        idx = jnp.min(jnp.where(x == m, lane, PEER_N_KEYS), axis=-1, keepdims=True)
        hit = lane == idx
        rank = jnp.where(hit, float(k), rank)
        x = jnp.where(hit, -jnp.inf, x)
        vals.append(m)
    return vals, rank


def _peer_prep_kernel(x_ref, g_ref, wq_ref, sk_ref, ht_ref, r2_ref, e2_ref, c_ref, cf_ref):
    h2 = _rms(x_ref[...], g_ref[...])
    ht_ref[...] = h2.T.astype(BF16)
    qry = jnp.dot(h2.astype(BF16), wq_ref[...], preferred_element_type=F32)
    tb = qry.shape[0]
    lane = lax.broadcasted_iota(jnp.int32, (tb, LANES), 1)
    lane_mod = lane % PEER_TOPK
    lane_div = lane // PEER_TOPK
    for h in range(PEER_HEADS):
        s1 = _dot_t(qry[:, (2 * h) * LANES:(2 * h + 1) * LANES].astype(BF16), sk_ref[2 * h])
        s2 = _dot_t(qry[:, (2 * h + 1) * LANES:(2 * h + 2) * LANES].astype(BF16), sk_ref[2 * h + 1])
        a, _ = _extract16(s1, lane)
        bb, r2 = _extract16(s2, lane)
        b_row = jnp.zeros((tb, LANES), F32)
        a_lo = jnp.zeros((tb, LANES), F32)
        a_hi = jnp.zeros((tb, LANES), F32)
        half = PEER_TOPK // 2
        for k in range(PEER_TOPK):
            b_row = jnp.where(lane_mod == k, bb[k], b_row)
            if k < half:
                a_lo = jnp.where(lane_div == k, a[k], a_lo)
            else:
                a_hi = jnp.where(lane_div == k - half, a[k], a_hi)
        c0 = a_lo + b_row
        c1 = a_hi + b_row
        zsum = jnp.zeros((tb, 1), F32)
        mtop = None
        m = None
        for k in range(PEER_TOPK):
            m = jnp.max(jnp.maximum(c0, c1), axis=-1, keepdims=True)
            if k == 0:
                mtop = m
            zsum = zsum + jnp.exp(m - mtop)
            idx = jnp.min(jnp.minimum(jnp.where(c0 == m, lane, 2 * LANES),
                                      jnp.where(c1 == m, lane + LANES, 2 * LANES)),
                          axis=-1, keepdims=True)
            c0 = jnp.where(lane == idx, -jnp.inf, c0)
            c1 = jnp.where(lane + LANES == idx, -jnp.inf, c1)
        thr = m
        cnt = jnp.zeros((tb, LANES), F32)
        for k in range(PEER_TOPK):
            cnt = cnt + jnp.where(s1 + bb[k] >= thr, 1.0, 0.0)
        coef = jnp.exp(s1 - a[0]) / zsum
        e2 = jnp.exp(s2 - bb[0])
        r2_ref[h] = r2.T
        e2_ref[h] = e2.T
        c_ref[h] = cnt.T
        cf_ref[h] = coef.T


def _peer_prep(x, gain, w_q, sub_keys, tb=256):
    T = x.shape[0]
    tb = min(tb, T)
    aux_spec = pl.BlockSpec((PEER_HEADS, PEER_N_KEYS, tb), lambda i: (0, 0, i))
    aux_shape = jax.ShapeDtypeStruct((PEER_HEADS, PEER_N_KEYS, T), F32)
    return pl.pallas_call(
        _peer_prep_kernel,
        grid=(T // tb,),
        in_specs=[pl.BlockSpec((tb, D_MODEL), lambda i: (i, 0)),
                  pl.BlockSpec((1, D_MODEL), lambda i: (0, 0)),
                  pl.BlockSpec((D_MODEL, 2 * PEER_HEADS * LANES), lambda i: (0, 0)),
                  pl.BlockSpec((2 * PEER_HEADS, PEER_N_KEYS, LANES), lambda i: (0, 0, 0))],
        out_specs=[pl.BlockSpec((D_MODEL, tb), lambda i: (0, i)),
                   aux_spec, aux_spec, aux_spec, aux_spec],
        out_shape=[jax.ShapeDtypeStruct((D_MODEL, T), BF16),
                   aux_shape, aux_shape, aux_shape, aux_shape],
        compiler_params=_cparams(("parallel",), VMEM_LIMIT),
        name="peer_prep",
    )(x, gain, w_q, sub_keys)


PEER_ET = 256


def _peer_dense_kernel(x_ref, ht_ref, r2_ref, e2_ref, c_ref, cf_ref, u_ref, vt_ref, o_ref, acc_scr):
    et = pl.program_id(1)

    @pl.when(et == 0)
    def _():
        acc_scr[...] = jnp.zeros_like(acc_scr)

    act = jax.nn.gelu(jnp.dot(u_ref[...], ht_ref[...], preferred_element_type=F32))
    halves = []
    for half in range(PEER_ET // PEER_N_KEYS):
        i = et * (PEER_ET // PEER_N_KEYS) + half
        g = None
        for h in range(PEER_HEADS):
            c_row = c_ref[h, pl.ds(i, 1), :]
            cf_row = cf_ref[h, pl.ds(i, 1), :]
            t = jnp.where(r2_ref[h] < c_row, e2_ref[h] * cf_row, 0.0)
            g = t if g is None else g + t
        halves.append(g)
    gate = jnp.concatenate(halves, axis=0)
    acc_scr[...] += jnp.dot(vt_ref[...], (gate * act).astype(BF16), preferred_element_type=F32)

    @pl.when(et == pl.num_programs(1) - 1)
    def _():
        o_ref[...] = x_ref[...] + acc_scr[...].T


def _peer_dense(x, ht, r2, e2, cnt, coef, u_tab, vt_tab, tb=512):
    T = x.shape[0]
    tb = min(tb, T)
    n_exp = u_tab.shape[0]
    aux_spec = pl.BlockSpec((PEER_HEADS, PEER_N_KEYS, tb), lambda i, e: (0, 0, i))
    return pl.pallas_call(
        _peer_dense_kernel,
        grid=(T // tb, n_exp // PEER_ET),
        in_specs=[pl.BlockSpec((tb, D_MODEL), lambda i, e: (i, 0)),
                  pl.BlockSpec((D_MODEL, tb), lambda i, e: (0, i)),
                  aux_spec, aux_spec, aux_spec, aux_spec,
                  pl.BlockSpec((PEER_ET, D_MODEL), lambda i, e: (e, 0)),
                  pl.BlockSpec((D_MODEL, PEER_ET), lambda i, e: (0, e))],
        out_specs=pl.BlockSpec((tb, D_MODEL), lambda i, e: (i, 0)),
        out_shape=jax.ShapeDtypeStruct((T, D_MODEL), F32),
        scratch_shapes=[pltpu.VMEM((D_MODEL, tb), F32)],
        compiler_params=_cparams(("parallel", "arbitrary"), VMEM_LIMIT),
        name="peer_dense",
    )(x, ht, r2, e2, cnt, coef, u_tab, vt_tab)


def _final_kernel(x_ref, g_ref, o_ref):
    o_ref[...] = _rms(x_ref[...], g_ref[...])


def _final_norm(x, gain, tb=1024):
    T = x.shape[0]
    tb = min(tb, T)
    return pl.pallas_call(
        _final_kernel,
        grid=(T // tb,),
        in_specs=[pl.BlockSpec((tb, D_MODEL), lambda i: (i, 0)),
                  pl.BlockSpec((1, D_MODEL), lambda i: (0, 0))],
        out_specs=pl.BlockSpec((tb, D_MODEL), lambda i: (i, 0)),
        out_shape=jax.ShapeDtypeStruct((T, D_MODEL), F32),
        compiler_params=_cparams(("parallel",)),
        name="final_norm",
    )(x, gain)


def _rope_tables(positions):
    pos = positions.astype(F32).reshape(-1, 1)

    def tables(dim):
        half = dim // 2
        inv = ROPE_THETA ** (-jnp.arange(half, dtype=F32) / half)
        ang = pos * inv
        cos = jnp.cos(ang)
        sin = jnp.sin(ang)
        reps = LANES // dim
        return (jnp.tile(jnp.concatenate([cos, cos], axis=-1), (1, reps)),
                jnp.tile(jnp.concatenate([-sin, sin], axis=-1), (1, reps)))

    cos_c, sin_c = tables(C_HEAD_DIM)
    cos_i, sin_i = tables(IDX_DIM)
    return cos_c, sin_c, cos_i, sin_i


def _pack_w_in(w):
    n_idx = IDX_DIM + IDX_HEADS
    main = w[:, :N_MAIN]
    idx = w[:, N_MAIN:N_MAIN + n_idx]
    gates = w[:, N_MAIN + n_idx:]
    pad = jnp.zeros((w.shape[0], LANES - n_idx), w.dtype)
    return jnp.concatenate([gates, main, idx, pad], axis=1).astype(BF16)


def _layer(x, l, B, S, tabs, tri, lbp_all, p):
    proj = _inproj(x, p["norm1_gain"][l][None, :], _pack_w_in(p["w_in"][l]))
    y_a = _hgrn(proj, lbp_all[l], p["hgrn_norm_gain"][l][None, :], B, S)
    bias_full = jnp.repeat(p["gmlp_b_s"][l].T, B_BLOCK, axis=1)
    y_b = _gmlp(proj, p["gmlp_norm_gain"][l][None, :], p["gmlp_w_s"][l], bias_full)
    q_r, k_r, v_b, qi_r, ki_r = _dsa_prep(proj, *tabs)
    y_c = _dsa(q_r, qi_r, proj, k_r, v_b, ki_r, tri, B, S)
    x = _merge(y_a, y_b, y_c, proj, p["w_branch"][l].astype(BF16), p["w_out"][l].astype(BF16), x)
    sk = p["peer_sub_keys"][l].reshape(2 * PEER_HEADS, PEER_N_KEYS, LANES).astype(BF16)
    ht, r2, e2, cnt, coef = _peer_prep(x, p["norm2_gain"][l][None, :],
                                       p["peer_w_q"][l].astype(BF16), sk)
    return _peer_dense(x, ht, r2, e2, cnt, coef, p["peer_u"][l].astype(BF16),
                       p["peer_v"][l].T.astype(BF16))


def kernel(x, positions, norm1_gain, w_in, hgrn_lb_logits, hgrn_norm_gain, gmlp_norm_gain,
           gmlp_w_s, gmlp_b_s, w_branch, w_out, norm2_gain, peer_w_q, peer_sub_keys,
           peer_u, peer_v, final_gain):
    B, S, _ = x.shape
    depth = w_in.shape[0]
    p = dict(norm1_gain=norm1_gain, w_in=w_in, hgrn_norm_gain=hgrn_norm_gain,
             gmlp_norm_gain=gmlp_norm_gain, gmlp_w_s=gmlp_w_s, gmlp_b_s=gmlp_b_s,
             w_branch=w_branch, w_out=w_out, norm2_gain=norm2_gain, peer_w_q=peer_w_q,
             peer_sub_keys=peer_sub_keys, peer_u=peer_u, peer_v=peer_v)
    sm = jax.nn.softmax(hgrn_lb_logits.astype(F32), axis=0)
    cs = jnp.cumsum(sm, axis=0)
    lb = cs - cs[0:1]
    lbp_all = jnp.stack([jnp.log(lb), jnp.log1p(-lb), 1.0 - lb], axis=1)
    tabs = _rope_tables(positions)
    r = lax.broadcasted_iota(jnp.int32, (DSA_TK, DSA_TK), 0)
    c = lax.broadcasted_iota(jnp.int32, (DSA_TK, DSA_TK), 1)
    tri = (r <= c).astype(BF16)
    xt = x.reshape(B * S, D_MODEL)
    for l in range(depth):
        xt = _layer(xt, l, B, S, tabs, tri, lbp_all, p)
    return _final_norm(xt, final_gain[None, :]).reshape(B, S, D_MODEL)
```

```python
import functools

import numpy as np
import jax
import jax.numpy as jnp
from jax import lax
from jax.experimental import pallas as pl
from jax.experimental.pallas import tpu as pltpu

F32 = jnp.float32
BF16 = jnp.bfloat16

D_MODEL = 1024
CHUNK = 64
EPS = 1e-6
ROPE_THETA = 10000.0
A_HEADS = 4
A_DK = 128
B_GROUPS = 4
B_BLOCK = 128
C_HEADS = 4
C_HEAD_DIM = 128
IDX_HEADS = 8
IDX_DIM = 64
DSA_TOPK_MAX = 256
Q_BLOCK = 128
N_BRANCH = 3
BRANCH_WIDTH = 512
PEER_HEADS = 8
PEER_N_KEYS = 128
PEER_TOPK = 16

LANES = 128
VMEM_LIMIT = 56 * 1024 * 1024

N_MAIN = 10 * 512
N_PACKED = N_BRANCH * D_MODEL + N_MAIN + LANES
COLBLK_MAIN = (N_BRANCH * D_MODEL) // 512
COLBLK_IDX = (N_BRANCH * D_MODEL + N_MAIN) // LANES

NEG_BIG = -1e30
INT_MIN = -2147483648


def _cparams(sem, vmem=None):
    return pltpu.CompilerParams(dimension_semantics=sem, vmem_limit_bytes=vmem)


def _rms(x, gain):
    return x * lax.rsqrt(jnp.mean(x * x, axis=-1, keepdims=True) + EPS) * gain


def _dot_t(a, b):
    return lax.dot_general(a, b, (((1,), (1,)), ((), ())), preferred_element_type=F32)


def _inproj_kernel(x_ref, g_ref, w_ref, o_ref, h_scr):
    @pl.when(pl.program_id(1) == 0)
    def _():
        h_scr[...] = _rms(x_ref[...], g_ref[...]).astype(BF16)

    o_ref[...] = jnp.dot(h_scr[...], w_ref[...], preferred_element_type=F32)


def _inproj(x, gain, w_packed, tm=1024, tn=640):
    T = x.shape[0]
    tm = min(tm, T)
    return pl.pallas_call(
        _inproj_kernel,
        grid=(T // tm, N_PACKED // tn),
        in_specs=[pl.BlockSpec((tm, D_MODEL), lambda i, j: (i, 0)),
                  pl.BlockSpec((1, D_MODEL), lambda i, j: (0, 0)),
                  pl.BlockSpec((D_MODEL, tn), lambda i, j: (0, j))],
        out_specs=pl.BlockSpec((tm, tn), lambda i, j: (i, j)),
        out_shape=jax.ShapeDtypeStruct((T, N_PACKED), F32),
        scratch_shapes=[pltpu.VMEM((tm, D_MODEL), BF16)],
        compiler_params=_cparams(("parallel", "arbitrary"), VMEM_LIMIT),
        name="inproj",
    )(x, gain, w_packed)


HG_C = 128
HG_SUB = 16


def _hgrn_kernel(q_ref, f_ref, i_ref, g_ref, lbp_ref, gain_ref, o_ref, st_scr):
    @pl.when(pl.program_id(1) == 0)
    def _():
        st_scr[...] = jnp.zeros_like(st_scr)

    C = HG_C
    row = lax.broadcasted_iota(jnp.int32, (C, 1), 0)
    rowm = lax.broadcasted_iota(jnp.int32, (C, C), 0)
    colm = lax.broadcasted_iota(jnp.int32, (C, C), 1)
    gain = gain_ref[...]

    for h in range(A_HEADS):
        sl = slice(h * A_DK, (h + 1) * A_DK)
        z = f_ref[:, sl]
        loglb = lbp_ref[0:1, sl]
        log1mlb = lbp_ref[1:2, sl]
        omlb = lbp_ref[2:3, sl]
        lsig = -(jnp.maximum(-z, 0.0) + jnp.log1p(jnp.exp(-jnp.abs(z))))
        cterm = log1mlb + lsig
        logf = jnp.maximum(loglb, cterm) + jnp.log1p(jnp.exp(-jnp.abs(loglb - cterm)))
        key = omlb * jax.nn.sigmoid(-z)
        qf = jax.nn.silu(q_ref[:, sl]) * (A_DK ** -0.5)
        vf = i_ref[:, sl]

        b = logf
        sh = 1
        while sh < C:
            b = b + jnp.where(row >= sh, pltpu.roll(b, sh, 0), 0.0)
            sh *= 2

        attn = jnp.zeros((C, C), F32)
        m = C // 2
        while m >= HG_SUB:
            ref = jnp.concatenate(
                [jnp.broadcast_to(b[j * 2 * m + m - 1:j * 2 * m + m, :], (2 * m, A_DK))
                 for j in range(C // (2 * m))], axis=0)
            qt = qf * jnp.exp(jnp.minimum(b - ref, 0.0))
            kt = key * jnp.exp(jnp.minimum(ref - b, 0.0))
            a = _dot_t(qt.astype(BF16), kt.astype(BF16))
            msk = ((rowm // (2 * m)) == (colm // (2 * m))) & ((rowm % (2 * m)) >= m) & ((colm % (2 * m)) < m)
            attn = attn + jnp.where(msk, a, 0.0)
            m //= 2

        nsub = C // HG_SUB
        for j in range(HG_SUB):
            kj = jnp.concatenate(
                [jnp.broadcast_to(key[i * HG_SUB + j:i * HG_SUB + j + 1, :], (HG_SUB, A_DK))
                 for i in range(nsub)], axis=0)
            bj = jnp.concatenate(
                [jnp.broadcast_to(b[i * HG_SUB + j:i * HG_SUB + j + 1, :], (HG_SUB, A_DK))
                 for i in range(nsub)], axis=0)
            d = jnp.exp(jnp.minimum(b - bj, 0.0)) * qf * kj
            red = jnp.sum(d, axis=-1, keepdims=True)
            hit = (colm == (rowm // HG_SUB) * HG_SUB + j) & ((rowm % HG_SUB) >= j)
            attn = jnp.where(hit, red, attn)

        st = st_scr[h]
        o = jnp.dot(attn.astype(BF16), vf.astype(BF16), preferred_element_type=F32)
        o = o + _dot_t((qf * jnp.exp(b)).astype(BF16), st.astype(BF16))
        blast = b[C - 1:C, :]
        kdec = key * jnp.exp(blast - b)
        upd = lax.dot_general(vf.astype(BF16), kdec.astype(BF16), (((0,), (0,)), ((), ())),
                              preferred_element_type=F32)
        st_scr[h] = st * jnp.exp(blast) + upd

        o = _rms(o, gain)
        o_ref[:, sl] = o * jax.nn.silu(g_ref[:, sl])


def _hgrn(proj, lbp, gain, B, S):
    nc = S // HG_C
    cb = COLBLK_MAIN

    def spec(k):
        return pl.BlockSpec((HG_C, 512), lambda b, c, k=k: (b * nc + c, cb + k))

    return pl.pallas_call(
        _hgrn_kernel,
        grid=(B, nc),
        in_specs=[spec(0), spec(1), spec(2), spec(3),
                  pl.BlockSpec((3, 512), lambda b, c: (0, 0)),
                  pl.BlockSpec((1, A_DK), lambda b, c: (0, 0))],
        out_specs=pl.BlockSpec((HG_C, 512), lambda b, c: (b * nc + c, 0)),
        out_shape=jax.ShapeDtypeStruct((B * S, 512), F32),
        scratch_shapes=[pltpu.VMEM((A_HEADS, A_DK, A_DK), F32)],
        compiler_params=_cparams(("parallel", "arbitrary")),
        name="hgrn",
    )(proj, proj, proj, proj, lbp, gain)


def _gmlp_kernel(u_ref, v_ref, g_ref, w_ref, b_ref, o_ref, *, nsub):
    r = lax.broadcasted_iota(jnp.int32, (B_BLOCK, B_BLOCK), 0)
    c = lax.broadcasted_iota(jnp.int32, (B_BLOCK, B_BLOCK), 1)
    tril = r >= c
    u = jax.nn.gelu(u_ref[...])
    v = _rms(jax.nn.gelu(v_ref[...]), g_ref[...]).astype(BF16)
    bias = b_ref[...]
    for g in range(B_GROUPS):
        w = jnp.where(tril, w_ref[g], 0.0).astype(BF16)
        cs = slice(g * LANES, (g + 1) * LANES)
        for s in range(nsub):
            rs = slice(s * B_BLOCK, (s + 1) * B_BLOCK)
            mixed = jnp.dot(w, v[rs, cs], preferred_element_type=F32) + bias[:, cs]
            o_ref[rs, cs] = u[rs, cs] * mixed


def _gmlp(proj, gain, w_s, bias_full, tb=512):
    T = proj.shape[0]
    tb = min(tb, T)
    cb = COLBLK_MAIN
    return pl.pallas_call(
        functools.partial(_gmlp_kernel, nsub=tb // B_BLOCK),
        grid=(T // tb,),
        in_specs=[pl.BlockSpec((tb, 512), lambda i: (i, cb + 4)),
                  pl.BlockSpec((tb, 512), lambda i: (i, cb + 5)),
                  pl.BlockSpec((1, 512), lambda i: (0, 0)),
                  pl.BlockSpec((B_GROUPS, B_BLOCK, B_BLOCK), lambda i: (0, 0, 0)),
                  pl.BlockSpec((B_BLOCK, 512), lambda i: (0, 0))],
        out_specs=pl.BlockSpec((tb, 512), lambda i: (i, 0)),
        out_shape=jax.ShapeDtypeStruct((T, 512), F32),
        compiler_params=_cparams(("parallel",)),
        name="gmlp",
    )(proj, proj, gain, w_s, bias_full)


def _dsa_prep_kernel(q_ref, k_ref, v_ref, iq_ref, idx_ref, cc_ref, sc_ref, ci_ref, si_ref,
                     qo_ref, ko_ref, vo_ref, qio_ref, kio_ref):
    cc = cc_ref[...]
    sc = sc_ref[...]
    ci = ci_ref[...]
    si = si_ref[...]
    lane = lax.broadcasted_iota(jnp.int32, ci.shape, 1)
    first_half = (lane % IDX_DIM) < (IDX_DIM // 2)

    def rope_i(x):
        partner = jnp.where(first_half, pltpu.roll(x, LANES - IDX_DIM // 2, 1),
                            pltpu.roll(x, IDX_DIM // 2, 1))
        return x * ci + partner * si

    for h in range(C_HEADS):
        sl = slice(h * LANES, (h + 1) * LANES)
        x = q_ref[:, sl]
        qo_ref[:, sl] = ((x * cc + pltpu.roll(x, C_HEAD_DIM // 2, 1) * sc)
                         * (C_HEAD_DIM ** -0.5)).astype(BF16)
        x = k_ref[:, sl]
        ko_ref[:, sl] = (x * cc + pltpu.roll(x, C_HEAD_DIM // 2, 1) * sc).astype(BF16)
    vo_ref[...] = v_ref[...].astype(BF16)
    for j in range(IDX_HEADS // 2):
        r = rope_i(iq_ref[:, j * LANES:(j + 1) * LANES])
        qio_ref[:, (2 * j) * LANES:(2 * j + 1) * LANES] = r.astype(BF16)
        qio_ref[:, (2 * j + 1) * LANES:(2 * j + 2) * LANES] = pltpu.roll(r, IDX_DIM, 1).astype(BF16)
    rk = rope_i(idx_ref[...])
    kio_ref[...] = jnp.where(lane < IDX_DIM, rk, 0.0).astype(BF16)


def _dsa_prep(proj, cos_c, sin_c, cos_i, sin_i, tb=512):
    T = proj.shape[0]
    tb = min(tb, T)
    cb = COLBLK_MAIN

    def pspec(k):
        return pl.BlockSpec((tb, 512), lambda i, k=k: (i, cb + k))

    tspec = pl.BlockSpec((tb, LANES), lambda i: (i, 0))
    return pl.pallas_call(
        _dsa_prep_kernel,
        grid=(T // tb,),
        in_specs=[pspec(6), pspec(7), pspec(8), pspec(9),
                  pl.BlockSpec((tb, LANES), lambda i: (i, COLBLK_IDX)),
                  tspec, tspec, tspec, tspec],
        out_specs=[pl.BlockSpec((tb, 512), lambda i: (i, 0)),
                   pl.BlockSpec((tb, 512), lambda i: (i, 0)),
                   pl.BlockSpec((tb, 512), lambda i: (i, 0)),
                   pl.BlockSpec((tb, IDX_HEADS * LANES), lambda i: (i, 0)),
                   pl.BlockSpec((tb, LANES), lambda i: (i, 0))],
        out_shape=[jax.ShapeDtypeStruct((T, 512), BF16),
                   jax.ShapeDtypeStruct((T, 512), BF16),
                   jax.ShapeDtypeStruct((T, 512), BF16),
                   jax.ShapeDtypeStruct((T, IDX_HEADS * LANES), BF16),
                   jax.ShapeDtypeStruct((T, LANES), BF16)],
        compiler_params=_cparams(("parallel",)),
        name="dsa_prep",
    )(proj, proj, proj, proj, proj, cos_c, sin_c, cos_i, sin_i)


DSA_TK = 512


def _sortable(score):
    score = jnp.where(score == 0.0, 0.0, score)
    bits = pltpu.bitcast(score, jnp.int32)
    return bits ^ ((bits >> 31) & 0x7FFFFFFF)


def _dsa_kernel(q_ref, qi_ref, idx_ref, k_ref, v_ref, ki_ref, tri_ref, o_ref,
                qidx_scr, wb_scr, key_scr, *, topk):
    TK = DSA_TK
    qb = pl.program_id(1)
    nkb = (qb * Q_BLOCK + Q_BLOCK + TK - 1) // TK
    wscale = (IDX_HEADS ** -0.5) * (IDX_DIM ** -0.5)

    iw = idx_ref[...]
    for h in range(IDX_HEADS):
        qidx_scr[h * Q_BLOCK:(h + 1) * Q_BLOCK, :] = qi_ref[:, h * LANES:(h + 1) * LANES]
        wb_scr[h * Q_BLOCK:(h + 1) * Q_BLOCK, :] = jnp.broadcast_to(
            iw[:, IDX_DIM + h:IDX_DIM + h + 1] * wscale, (Q_BLOCK, LANES))

    qchunk = (qb * Q_BLOCK + lax.broadcasted_iota(jnp.int32, (Q_BLOCK, 1), 0)) // CHUNK
    lane_tk = lax.broadcasted_iota(jnp.int32, (Q_BLOCK, TK), 1)

    def p1(kb, carry):
        ki = ki_ref[pl.ds(pl.multiple_of(kb * TK, TK), TK), :]
        logits = _dot_t(qidx_scr[...], ki)
        tiles = []
        for c in range(TK // LANES):
            s = jnp.zeros((Q_BLOCK, LANES), F32)
            for h in range(IDX_HEADS):
                lg = logits[h * Q_BLOCK:(h + 1) * Q_BLOCK, c * LANES:(c + 1) * LANES]
                s = s + jnp.maximum(lg, 0.0) * wb_scr[h * Q_BLOCK:(h + 1) * Q_BLOCK, :]
            tiles.append(s)
        score = jnp.concatenate(tiles, axis=1)
        adm = ((kb * TK + lane_tk) // CHUNK) <= qchunk
        score = jnp.where(adm, score, -jnp.inf)
        key_scr[kb] = _sortable(score)
        return carry

    lax.fori_loop(0, nkb, p1, 0)

    def count_ge(cand):
        def body(kb, acc):
            kt = key_scr[kb]
            for c in range(TK // LANES):
                acc = acc + jnp.where(kt[:, c * LANES:(c + 1) * LANES] >= cand, 1, 0)
            return acc
        acc = lax.fori_loop(0, nkb, body, jnp.zeros((Q_BLOCK, LANES), jnp.int32))
        return jnp.sum(acc, axis=-1, keepdims=True)

    zero = jnp.zeros((Q_BLOCK, 1), jnp.int32)
    base = jnp.where(count_ge(zero) >= topk, zero, zero + INT_MIN)

    def bis(i, base):
        cand = base | (jnp.int32(1) << (30 - i))
        return jnp.where(count_ge(cand) >= topk, cand, base)

    tau = lax.fori_loop(0, 31, bis, base)
    need = (topk - count_ge(tau + 1)).astype(F32)

    key_neg_inf = _sortable(jnp.full((1, 1), -jnp.inf, F32))

    def p2(kb, carry):
        eq_seen, ms, ls, accs = carry
        off = pl.multiple_of(kb * TK, TK)
        kt = key_scr[kb]
        eq = kt == tau
        prefix = jnp.dot(jnp.where(eq, 1.0, 0.0).astype(BF16), tri_ref[...],
                         preferred_element_type=F32)
        sel = (kt > tau) | (eq & ((eq_seen + prefix) <= need))
        sel = sel & (kt > key_neg_inf)
        eq_seen = eq_seen + prefix[:, TK - 1:TK]
        new_ms, new_ls, new_accs = [], [], []
        for h in range(C_HEADS):
            sl = slice(h * LANES, (h + 1) * LANES)
            s = _dot_t(q_ref[:, sl], k_ref[pl.ds(off, TK), sl])
            s = jnp.where(sel, s, NEG_BIG)
            m_new = jnp.maximum(ms[h], jnp.max(s, axis=-1, keepdims=True))
            alpha = jnp.exp(ms[h] - m_new)
            p = jnp.where(sel, jnp.exp(s - m_new), 0.0)
            new_ls.append(alpha * ls[h] + jnp.sum(p, axis=-1, keepdims=True))
            new_accs.append(alpha * accs[h] + jnp.dot(p.astype(BF16), v_ref[pl.ds(off, TK), sl],
                                                       preferred_element_type=F32))
            new_ms.append(m_new)
        return eq_seen, tuple(new_ms), tuple(new_ls), tuple(new_accs)

    col0 = jnp.zeros((Q_BLOCK, 1), F32)
    init = (col0,
            tuple(col0 + NEG_BIG for _ in range(C_HEADS)),
            tuple(col0 for _ in range(C_HEADS)),
            tuple(jnp.zeros((Q_BLOCK, LANES), F32) for _ in range(C_HEADS)))
    _, ms, ls, accs = lax.fori_loop(0, nkb, p2, init)
    for h in range(C_HEADS):
        o_ref[:, h * LANES:(h + 1) * LANES] = accs[h] / ls[h]


def _dsa(q_r, qi_r, proj, k_r, v_b, ki_r, tri, B, S):
    nq = S // Q_BLOCK
    topk = min(DSA_TOPK_MAX, S // 4)
    nkb_max = (S + DSA_TK - 1) // DSA_TK
    Sp = nkb_max * DSA_TK
    assert Sp == S, "sequence length must be a multiple of the key tile"
    return pl.pallas_call(
        functools.partial(_dsa_kernel, topk=topk),
        grid=(B, nq),
        in_specs=[pl.BlockSpec((Q_BLOCK, 512), lambda b, q: (b * nq + q, 0)),
                  pl.BlockSpec((Q_BLOCK, IDX_HEADS * LANES), lambda b, q: (b * nq + q, 0)),
                  pl.BlockSpec((Q_BLOCK, LANES), lambda b, q: (b * nq + q, COLBLK_IDX)),
                  pl.BlockSpec((S, 512), lambda b, q: (b, 0)),
                  pl.BlockSpec((S, 512), lambda b, q: (b, 0)),
                  pl.BlockSpec((S, LANES), lambda b, q: (b, 0)),
                  pl.BlockSpec((DSA_TK, DSA_TK), lambda b, q: (0, 0))],
        out_specs=pl.BlockSpec((Q_BLOCK, 512), lambda b, q: (b * nq + q, 0)),
        out_shape=jax.ShapeDtypeStruct((B * S, 512), F32),
        scratch_shapes=[pltpu.VMEM((IDX_HEADS * Q_BLOCK, LANES), BF16),
                        pltpu.VMEM((IDX_HEADS * Q_BLOCK, LANES), F32),
                        pltpu.VMEM((nkb_max, Q_BLOCK, DSA_TK), jnp.int32)],
        compiler_params=_cparams(("parallel", "arbitrary"), VMEM_LIMIT),
        name="dsa",
    )(q_r, qi_r, proj, k_r, v_b, ki_r, tri)


def _merge_kernel(ya_ref, yb_ref, yc_ref, g0_ref, g1_ref, g2_ref, wb_ref, wo_ref, x_ref, o_ref):
    mixed = None
    for y_ref, g_ref, n in ((ya_ref, g0_ref, 0), (yb_ref, g1_ref, 1), (yc_ref, g2_ref, 2)):
        bp = jnp.dot(y_ref[...].astype(BF16), wb_ref[n], preferred_element_type=F32)
        t = jax.nn.sigmoid(g_ref[...]) * bp
        mixed = t if mixed is None else mixed + t
    o_ref[...] = x_ref[...] + jnp.dot(mixed.astype(BF16), wo_ref[...], preferred_element_type=F32)


def _merge(y_a, y_b, y_c, proj, w_branch, w_out, x, tb=256):
    T = x.shape[0]
    tb = min(tb, T)
    yspec = pl.BlockSpec((tb, 512), lambda i: (i, 0))

    def gspec(n):
        return pl.BlockSpec((tb, D_MODEL), lambda i, n=n: (i, n))

    return pl.pallas_call(
        _merge_kernel,
        grid=(T // tb,),
        in_specs=[yspec, yspec, yspec, gspec(0), gspec(1), gspec(2),
                  pl.BlockSpec((N_BRANCH, BRANCH_WIDTH, D_MODEL), lambda i: (0, 0, 0)),
                  pl.BlockSpec((D_MODEL, D_MODEL), lambda i: (0, 0)),
                  pl.BlockSpec((tb, D_MODEL), lambda i: (i, 0))],
        out_specs=pl.BlockSpec((tb, D_MODEL), lambda i: (i, 0)),
        out_shape=jax.ShapeDtypeStruct((T, D_MODEL), F32),
        compiler_params=_cparams(("parallel",), VMEM_LIMIT),
        name="merge",
    )(y_a, y_b, y_c, proj, proj, proj, w_branch, w_out, x)


def _oddeven_merge(lo, hi, r):
    step = r * 2
    if step < hi - lo:
        yield from _oddeven_merge(lo, hi, step)
        yield from _oddeven_merge(lo + r, hi, step)
        yield from [(i, i + r) for i in range(lo + r, hi - r, step)]
    else:
        yield (lo, lo + r)


def _oddeven_sort(lo, hi):
    if hi - lo >= 1:
        mid = lo + (hi - lo) // 2
        yield from _oddeven_sort(lo, mid)
        yield from _oddeven_sort(mid + 1, hi)
        yield from _oddeven_merge(lo, hi, 1)


SUBLANES = 8
NET16 = tuple(_oddeven_sort(0, PEER_TOPK - 1))
NET8 = tuple(_oddeven_sort(0, SUBLANES - 1))
PEER_CAND = tuple((k1, k2) for k1 in range(PEER_TOPK) for k2 in range(PEER_TOPK)
                  if (k1 + 1) * (k2 + 1) <= PEER_TOPK)


def _ce(x, i, j):
    a, b = x[i], x[j]
    x[i] = jnp.maximum(a, b)
    x[j] = jnp.minimum(a, b)


def _bitonic_sort16(m):
    d = PEER_TOPK // 2
    while d >= 1:
        for i in range(PEER_TOPK):
            if (i & d) == 0:
                _ce(m, i, i + d)
        d //= 2


def _merge_sublanes(v, shifts):
    for sh in shifts:
        w = [pltpu.roll(t, sh, 0) for t in v]
        v = [jnp.maximum(v[k], w[PEER_TOPK - 1 - k]) for k in range(PEER_TOPK)]
        _bitonic_sort16(v)
    return v


def _top16_of_rows(s):
    v = [s[SUBLANES * k:SUBLANES * (k + 1), :] for k in range(PEER_N_KEYS // SUBLANES)]
    for i, j in NET16:
        _ce(v, i, j)
    return _merge_sublanes(v, (4, 2, 1))


def _peer_prep_kernel(x_ref, g_ref, wqt_ref, sk_ref, ht_ref, r2_ref, e2_ref, c_ref, cf_ref, hb_scr):
    tb = x_ref.shape[0]
    h2 = _rms(x_ref[...], g_ref[...])
    hb_scr[...] = h2.T.astype(BF16)
    ht_ref[...] = hb_scr[...]
    sub = lax.broadcasted_iota(jnp.int32, (SUBLANES, LANES), 0)
    neg = jnp.full((SUBLANES, LANES), -jnp.inf, F32)

    def head(h, carry):
        hb = hb_scr[...]
        q1 = jnp.dot(wqt_ref[pl.ds(pl.multiple_of(h * 2 * LANES, LANES), LANES), :], hb,
                     preferred_element_type=F32)
        q2 = jnp.dot(wqt_ref[pl.ds(pl.multiple_of(h * 2 * LANES + LANES, LANES), LANES), :], hb,
                     preferred_element_type=F32)
        s1f = jnp.dot(sk_ref[2 * h], q1.astype(BF16), preferred_element_type=F32)
        s2f = jnp.dot(sk_ref[2 * h + 1], q2.astype(BF16), preferred_element_type=F32)
        for lt in range(tb // LANES):
            ls = slice(lt * LANES, (lt + 1) * LANES)
            s1 = s1f[:, ls]
            s2 = s2f[:, ls]
            a = _top16_of_rows(s1)
            b = _top16_of_rows(s2)
            packed = []
            for v in range((len(PEER_CAND) + SUBLANES - 1) // SUBLANES):
                p = neg
                for s in range(SUBLANES):
                    c = SUBLANES * v + s
                    if c < len(PEER_CAND):
                        k1, k2 = PEER_CAND[c]
                        p = jnp.where(sub == s, a[k1] + b[k2], p)
                packed.append(p)
            while len(packed) < SUBLANES:
                packed.append(neg)
            for i, j in NET8:
                _ce(packed, i, j)
            w = [pltpu.roll(t, 4, 0) for t in packed]
            m = packed + w[::-1]
            _bitonic_sort16(m)
            m = _merge_sublanes(m, (2, 1))
            thr = m[PEER_TOPK - 1][0:1, :]
            zsum = m[0] - m[0] + 1.0
            for k in range(1, PEER_TOPK):
                zsum = zsum + jnp.exp(m[k] - m[0])
            zinv = 1.0 / zsum[0:1, :]
            cnt = jnp.zeros((PEER_N_KEYS, LANES), F32)
            rank = jnp.zeros((PEER_N_KEYS, LANES), F32)
            for k in range(PEER_TOPK):
                bk = b[k][0:1, :]
                cnt = cnt + jnp.where(s1 + bk >= thr, 1.0, 0.0)
                rank = rank + jnp.where(bk > s2, 1.0, 0.0)
            r2_ref[h, :, ls] = rank.astype(r2_ref.dtype)
            e2_ref[h, :, ls] = jnp.exp(s2 - b[0][0:1, :]).astype(e2_ref.dtype)
            c_ref[h, :, ls] = cnt
            cf_ref[h, :, ls] = jnp.exp(s1 - a[0][0:1, :]) * zinv
        return carry

    lax.fori_loop(0, PEER_HEADS, head, 0)


def _peer_prep(x, gain, w_qt, sub_keys, tb=256):
    T = x.shape[0]
    tb = min(tb, T)
    aux_spec = pl.BlockSpec((PEER_HEADS, PEER_N_KEYS, tb), lambda i: (0, 0, i))
    aux_f32 = jax.ShapeDtypeStruct((PEER_HEADS, PEER_N_KEYS, T), F32)
    aux_b16 = jax.ShapeDtypeStruct((PEER_HEADS, PEER_N_KEYS, T), BF16)
    return pl.pallas_call(
        _peer_prep_kernel,
        grid=(T // tb,),
        in_specs=[pl.BlockSpec((tb, D_MODEL), lambda i: (i, 0)),
                  pl.BlockSpec((1, D_MODEL), lambda i: (0, 0)),
                  pl.BlockSpec((2 * PEER_HEADS * LANES, D_MODEL), lambda i: (0, 0)),
                  pl.BlockSpec((2 * PEER_HEADS, PEER_N_KEYS, LANES), lambda i: (0, 0, 0))],
        out_specs=[pl.BlockSpec((D_MODEL, tb), lambda i: (0, i)),
                   aux_spec, aux_spec, aux_spec, aux_spec],
        out_shape=[jax.ShapeDtypeStruct((D_MODEL, T), BF16),
                   aux_b16, aux_b16, aux_f32, aux_f32],
        scratch_shapes=[pltpu.VMEM((D_MODEL, tb), BF16)],
        compiler_params=_cparams(("parallel",), VMEM_LIMIT),
        name="peer_prep",
    )(x, gain, w_qt, sub_keys)


PEER_ET = 512


def _peer_dense_kernel(x_ref, ht_ref, r2_ref, e2_ref, c_ref, cf_ref, u_ref, vt_ref, o_ref,
                       acc_scr, ga_scr):
    et = pl.program_id(1)
    n_sub = PEER_ET // PEER_N_KEYS

    @pl.when(et == 0)
    def _():
        acc_scr[...] = jnp.zeros_like(acc_scr)

    hb = ht_ref[...]
    for half in range(n_sub):
        rs = slice(half * PEER_N_KEYS, (half + 1) * PEER_N_KEYS)
        i = et * n_sub + half
        act = jax.nn.gelu(jnp.dot(u_ref[rs, :], hb, preferred_element_type=F32))
        g = None
        for h in range(PEER_HEADS):
            c_row = c_ref[h, pl.ds(i, 1), :].astype(r2_ref.dtype)
            cf_row = cf_ref[h, pl.ds(i, 1), :].astype(e2_ref.dtype)
            t = jnp.where(r2_ref[h] < c_row, e2_ref[h] * cf_row, jnp.zeros((), e2_ref.dtype))
            g = t if g is None else g + t
        ga_scr[rs, :] = (g.astype(F32) * act).astype(BF16)
    acc_scr[...] += jnp.dot(vt_ref[...], ga_scr[...], preferred_element_type=F32)

    @pl.when(et == pl.num_programs(1) - 1)
    def _():
        o_ref[...] = x_ref[...] + acc_scr[...].T


def _peer_dense(x, ht, r2, e2, cnt, coef, u_tab, vt_tab, tb=512):
    T = x.shape[0]
    tb = min(tb, T)
    n_exp = u_tab.shape[0]
    aux_spec = pl.BlockSpec((PEER_HEADS, PEER_N_KEYS, tb), lambda i, e: (0, 0, i))
    return pl.pallas_call(
        _peer_dense_kernel,
        grid=(T // tb, n_exp // PEER_ET),
        in_specs=[pl.BlockSpec((tb, D_MODEL), lambda i, e: (i, 0)),
                  pl.BlockSpec((D_MODEL, tb), lambda i, e: (0, i)),
                  aux_spec, aux_spec, aux_spec, aux_spec,
                  pl.BlockSpec((PEER_ET, D_MODEL), lambda i, e: (e, 0)),
                  pl.BlockSpec((D_MODEL, PEER_ET), lambda i, e: (0, e))],
        out_specs=pl.BlockSpec((tb, D_MODEL), lambda i, e: (i, 0)),
        out_shape=jax.ShapeDtypeStruct((T, D_MODEL), F32),
        scratch_shapes=[pltpu.VMEM((D_MODEL, tb), F32),
                        pltpu.VMEM((PEER_ET, tb), BF16)],
        compiler_params=_cparams(("parallel", "arbitrary"), VMEM_LIMIT),
        name="peer_dense",
    )(x, ht, r2, e2, cnt, coef, u_tab, vt_tab)


def _final_kernel(x_ref, g_ref, o_ref):
    o_ref[...] = _rms(x_ref[...], g_ref[...])


def _final_norm(x, gain, tb=1024):
    T = x.shape[0]
    tb = min(tb, T)
    return pl.pallas_call(
        _final_kernel,
        grid=(T // tb,),
        in_specs=[pl.BlockSpec((tb, D_MODEL), lambda i: (i, 0)),
                  pl.BlockSpec((1, D_MODEL), lambda i: (0, 0))],
        out_specs=pl.BlockSpec((tb, D_MODEL), lambda i: (i, 0)),
        out_shape=jax.ShapeDtypeStruct((T, D_MODEL), F32),
        compiler_params=_cparams(("parallel",)),
        name="final_norm",
    )(x, gain)


def _rope_tables(positions):
    pos = positions.astype(F32).reshape(-1, 1)

    def tables(dim):
        half = dim // 2
        inv = ROPE_THETA ** (-jnp.arange(half, dtype=F32) / half)
        ang = pos * inv
        cos = jnp.cos(ang)
        sin = jnp.sin(ang)
        reps = LANES // dim
        return (jnp.tile(jnp.concatenate([cos, cos], axis=-1), (1, reps)),
                jnp.tile(jnp.concatenate([-sin, sin], axis=-1), (1, reps)))

    cos_c, sin_c = tables(C_HEAD_DIM)
    cos_i, sin_i = tables(IDX_DIM)
    return cos_c, sin_c, cos_i, sin_i


def _pack_w_in(w):
    n_idx = IDX_DIM + IDX_HEADS
    main = w[:, :N_MAIN]
    idx = w[:, N_MAIN:N_MAIN + n_idx]
    gates = w[:, N_MAIN + n_idx:]
    pad = jnp.zeros((w.shape[0], LANES - n_idx), w.dtype)
    return jnp.concatenate([gates, main, idx, pad], axis=1).astype(BF16)


def _layer(x, l, B, S, tabs, tri, lbp_all, p):
    proj = _inproj(x, p["norm1_gain"][l][None, :], _pack_w_in(p["w_in"][l]))
    y_a = _hgrn(proj, lbp_all[l], p["hgrn_norm_gain"][l][None, :], B, S)
    bias_full = jnp.repeat(p["gmlp_b_s"][l].T, B_BLOCK, axis=1)
    y_b = _gmlp(proj, p["gmlp_norm_gain"][l][None, :], p["gmlp_w_s"][l], bias_full)
    q_r, k_r, v_b, qi_r, ki_r = _dsa_prep(proj, *tabs)
    y_c = _dsa(q_r, qi_r, proj, k_r, v_b, ki_r, tri, B, S)
    x = _merge(y_a, y_b, y_c, proj, p["w_branch"][l].astype(BF16), p["w_out"][l].astype(BF16), x)
    sk = p["peer_sub_keys"][l].reshape(2 * PEER_HEADS, PEER_N_KEYS, LANES).astype(BF16)
    ht, r2, e2, cnt, coef = _peer_prep(x, p["norm2_gain"][l][None, :],
                                       p["peer_w_q"][l].T.astype(BF16), sk)
    return _peer_dense(x, ht, r2, e2, cnt, coef, p["peer_u"][l].astype(BF16),
                       p["peer_v"][l].T.astype(BF16))


def kernel(x, positions, norm1_gain, w_in, hgrn_lb_logits, hgrn_norm_gain, gmlp_norm_gain,
           gmlp_w_s, gmlp_b_s, w_branch, w_out, norm2_gain, peer_w_q, peer_sub_keys,
           peer_u, peer_v, final_gain):
    B, S, _ = x.shape
    depth = w_in.shape[0]
    p = dict(norm1_gain=norm1_gain, w_in=w_in, hgrn_norm_gain=hgrn_norm_gain,
             gmlp_norm_gain=gmlp_norm_gain, gmlp_w_s=gmlp_w_s, gmlp_b_s=gmlp_b_s,
             w_branch=w_branch, w_out=w_out, norm2_gain=norm2_gain, peer_w_q=peer_w_q,
             peer_sub_keys=peer_sub_keys, peer_u=peer_u, peer_v=peer_v)
    sm = jax.nn.softmax(hgrn_lb_logits.astype(F32), axis=0)
    cs = jnp.cumsum(sm, axis=0)
    lb = cs - cs[0:1]
    lbp_all = jnp.stack([jnp.log(lb), jnp.log1p(-lb), 1.0 - lb], axis=1)
    tabs = _rope_tables(positions)
    r = lax.broadcasted_iota(jnp.int32, (DSA_TK, DSA_TK), 0)
    c = lax.broadcasted_iota(jnp.int32, (DSA_TK, DSA_TK), 1)
    tri = (r <= c).astype(BF16)
    xt = x.reshape(B * S, D_MODEL)
    for l in range(depth):
        xt = _layer(xt, l, B, S, tabs, tri, lbp_all, p)
    return _final_norm(xt, final_gain[None, :]).reshape(B, S, D_MODEL)
```

```python
import functools

import numpy as np
import jax
import jax.numpy as jnp
from jax import lax
from jax.experimental import pallas as pl
from jax.experimental.pallas import tpu as pltpu

F32 = jnp.float32
BF16 = jnp.bfloat16

D_MODEL = 1024
CHUNK = 64
EPS = 1e-6
ROPE_THETA = 10000.0
A_HEADS = 4
A_DK = 128
B_GROUPS = 4
B_BLOCK = 128
C_HEADS = 4
C_HEAD_DIM = 128
IDX_HEADS = 8
IDX_DIM = 64
DSA_TOPK_MAX = 256
Q_BLOCK = 128
N_BRANCH = 3
BRANCH_WIDTH = 512
PEER_HEADS = 8
PEER_N_KEYS = 128
PEER_TOPK = 16

LANES = 128
VMEM_LIMIT = 56 * 1024 * 1024

N_MAIN = 10 * 512
N_PACKED = N_BRANCH * D_MODEL + N_MAIN + LANES
COLBLK_MAIN = (N_BRANCH * D_MODEL) // 512
COLBLK_IDX = (N_BRANCH * D_MODEL + N_MAIN) // LANES

NEG_BIG = -1e30
INT_MIN = -2147483648


def _cparams(sem, vmem=None):
    return pltpu.CompilerParams(dimension_semantics=sem, vmem_limit_bytes=vmem)


def _rms(x, gain):
    return x * lax.rsqrt(jnp.mean(x * x, axis=-1, keepdims=True) + EPS) * gain


def _dot_t(a, b):
    return lax.dot_general(a, b, (((1,), (1,)), ((), ())), preferred_element_type=F32)


def _inproj_kernel(x_ref, g_ref, w_ref, o_ref, h_scr):
    @pl.when(pl.program_id(1) == 0)
    def _():
        h_scr[...] = _rms(x_ref[...], g_ref[...]).astype(BF16)

    o_ref[...] = jnp.dot(h_scr[...], w_ref[...], preferred_element_type=F32)


def _inproj(x, gain, w_packed, tm=1024, tn=640):
    T = x.shape[0]
    tm = min(tm, T)
    return pl.pallas_call(
        _inproj_kernel,
        grid=(T // tm, N_PACKED // tn),
        in_specs=[pl.BlockSpec((tm, D_MODEL), lambda i, j: (i, 0)),
                  pl.BlockSpec((1, D_MODEL), lambda i, j: (0, 0)),
                  pl.BlockSpec((D_MODEL, tn), lambda i, j: (0, j))],
        out_specs=pl.BlockSpec((tm, tn), lambda i, j: (i, j)),
        out_shape=jax.ShapeDtypeStruct((T, N_PACKED), F32),
        scratch_shapes=[pltpu.VMEM((tm, D_MODEL), BF16)],
        compiler_params=_cparams(("parallel", "arbitrary"), VMEM_LIMIT),
        name="inproj",
    )(x, gain, w_packed)


HG_C = 128
HG_SUB = 16


def _hgrn_kernel(q_ref, f_ref, i_ref, g_ref, lbp_ref, gain_ref, o_ref, st_scr):
    @pl.when(pl.program_id(1) == 0)
    def _():
        st_scr[...] = jnp.zeros_like(st_scr)

    C = HG_C
    row = lax.broadcasted_iota(jnp.int32, (C, 1), 0)
    rowm = lax.broadcasted_iota(jnp.int32, (C, C), 0)
    colm = lax.broadcasted_iota(jnp.int32, (C, C), 1)
    gain = gain_ref[...]

    for h in range(A_HEADS):
        sl = slice(h * A_DK, (h + 1) * A_DK)
        z = f_ref[:, sl]
        loglb = lbp_ref[0:1, sl]
        log1mlb = lbp_ref[1:2, sl]
        omlb = lbp_ref[2:3, sl]
        lsig = -(jnp.maximum(-z, 0.0) + jnp.log1p(jnp.exp(-jnp.abs(z))))
        cterm = log1mlb + lsig
        logf = jnp.maximum(loglb, cterm) + jnp.log1p(jnp.exp(-jnp.abs(loglb - cterm)))
        key = omlb * jax.nn.sigmoid(-z)
        qf = jax.nn.silu(q_ref[:, sl]) * (A_DK ** -0.5)
        vf = i_ref[:, sl]

        b = logf
        sh = 1
        while sh < C:
            b = b + jnp.where(row >= sh, pltpu.roll(b, sh, 0), 0.0)
            sh *= 2

        attn = jnp.zeros((C, C), F32)
        m = C // 2
        while m >= HG_SUB:
            ref = jnp.concatenate(
                [jnp.broadcast_to(b[j * 2 * m + m - 1:j * 2 * m + m, :], (2 * m, A_DK))
                 for j in range(C // (2 * m))], axis=0)
            qt = qf * jnp.exp(jnp.minimum(b - ref, 0.0))
            kt = key * jnp.exp(jnp.minimum(ref - b, 0.0))
            a = _dot_t(qt.astype(BF16), kt.astype(BF16))
            msk = ((rowm // (2 * m)) == (colm // (2 * m))) & ((rowm % (2 * m)) >= m) & ((colm % (2 * m)) < m)
            attn = attn + jnp.where(msk, a, 0.0)
            m //= 2

        nsub = C // HG_SUB
        for j in range(HG_SUB):
            kj = jnp.concatenate(
                [jnp.broadcast_to(key[i * HG_SUB + j:i * HG_SUB + j + 1, :], (HG_SUB, A_DK))
                 for i in range(nsub)], axis=0)
            bj = jnp.concatenate(
                [jnp.broadcast_to(b[i * HG_SUB + j:i * HG_SUB + j + 1, :], (HG_SUB, A_DK))
                 for i in range(nsub)], axis=0)
            d = jnp.exp(jnp.minimum(b - bj, 0.0)) * qf * kj
            red = jnp.sum(d, axis=-1, keepdims=True)
            hit = (colm == (rowm // HG_SUB) * HG_SUB + j) & ((rowm % HG_SUB) >= j)
            attn = jnp.where(hit, red, attn)

        st = st_scr[h]
        o = jnp.dot(attn.astype(BF16), vf.astype(BF16), preferred_element_type=F32)
        o = o + _dot_t((qf * jnp.exp(b)).astype(BF16), st.astype(BF16))
        blast = b[C - 1:C, :]
        kdec = key * jnp.exp(blast - b)
        upd = lax.dot_general(vf.astype(BF16), kdec.astype(BF16), (((0,), (0,)), ((), ())),
                              preferred_element_type=F32)
        st_scr[h] = st * jnp.exp(blast) + upd

        o = _rms(o, gain)
        o_ref[:, sl] = o * jax.nn.silu(g_ref[:, sl])


def _hgrn(proj, lbp, gain, B, S):
    nc = S // HG_C
    cb = COLBLK_MAIN

    def spec(k):
        return pl.BlockSpec((HG_C, 512), lambda b, c, k=k: (b * nc + c, cb + k))

    return pl.pallas_call(
        _hgrn_kernel,
        grid=(B, nc),
        in_specs=[spec(0), spec(1), spec(2), spec(3),
                  pl.BlockSpec((3, 512), lambda b, c: (0, 0)),
                  pl.BlockSpec((1, A_DK), lambda b, c: (0, 0))],
        out_specs=pl.BlockSpec((HG_C, 512), lambda b, c: (b * nc + c, 0)),
        out_shape=jax.ShapeDtypeStruct((B * S, 512), F32),
        scratch_shapes=[pltpu.VMEM((A_HEADS, A_DK, A_DK), F32)],
        compiler_params=_cparams(("parallel", "arbitrary")),
        name="hgrn",
    )(proj, proj, proj, proj, lbp, gain)


def _gmlp_kernel(u_ref, v_ref, g_ref, w_ref, b_ref, o_ref, *, nsub):
    r = lax.broadcasted_iota(jnp.int32, (B_BLOCK, B_BLOCK), 0)
    c = lax.broadcasted_iota(jnp.int32, (B_BLOCK, B_BLOCK), 1)
    tril = r >= c
    u = jax.nn.gelu(u_ref[...])
    v = _rms(jax.nn.gelu(v_ref[...]), g_ref[...]).astype(BF16)
    bias = b_ref[...]
    for g in range(B_GROUPS):
        w = jnp.where(tril, w_ref[g], 0.0).astype(BF16)
        cs = slice(g * LANES, (g + 1) * LANES)
        for s in range(nsub):
            rs = slice(s * B_BLOCK, (s + 1) * B_BLOCK)
            mixed = jnp.dot(w, v[rs, cs], preferred_element_type=F32) + bias[:, cs]
            o_ref[rs, cs] = u[rs, cs] * mixed


def _gmlp(proj, gain, w_s, bias_full, tb=512):
    T = proj.shape[0]
    tb = min(tb, T)
    cb = COLBLK_MAIN
    return pl.pallas_call(
        functools.partial(_gmlp_kernel, nsub=tb // B_BLOCK),
        grid=(T // tb,),
        in_specs=[pl.BlockSpec((tb, 512), lambda i: (i, cb + 4)),
                  pl.BlockSpec((tb, 512), lambda i: (i, cb + 5)),
                  pl.BlockSpec((1, 512), lambda i: (0, 0)),
                  pl.BlockSpec((B_GROUPS, B_BLOCK, B_BLOCK), lambda i: (0, 0, 0)),
                  pl.BlockSpec((B_BLOCK, 512), lambda i: (0, 0))],
        out_specs=pl.BlockSpec((tb, 512), lambda i: (i, 0)),
        out_shape=jax.ShapeDtypeStruct((T, 512), F32),
        compiler_params=_cparams(("parallel",)),
        name="gmlp",
    )(proj, proj, gain, w_s, bias_full)


def _dsa_prep_kernel(q_ref, k_ref, v_ref, iq_ref, idx_ref, cc_ref, sc_ref, ci_ref, si_ref,
                     qo_ref, ko_ref, vo_ref, qio_ref, kio_ref):
    cc = cc_ref[...]
    sc = sc_ref[...]
    ci = ci_ref[...]
    si = si_ref[...]
    lane = lax.broadcasted_iota(jnp.int32, ci.shape, 1)
    first_half = (lane % IDX_DIM) < (IDX_DIM // 2)

    def rope_i(x):
        partner = jnp.where(first_half, pltpu.roll(x, LANES - IDX_DIM // 2, 1),
                            pltpu.roll(x, IDX_DIM // 2, 1))
        return x * ci + partner * si

    for h in range(C_HEADS):
        sl = slice(h * LANES, (h + 1) * LANES)
        x = q_ref[:, sl]
        qo_ref[:, sl] = ((x * cc + pltpu.roll(x, C_HEAD_DIM // 2, 1) * sc)
                         * (C_HEAD_DIM ** -0.5)).astype(BF16)
        x = k_ref[:, sl]
        ko_ref[:, sl] = (x * cc + pltpu.roll(x, C_HEAD_DIM // 2, 1) * sc).astype(BF16)
    vo_ref[0] = v_ref[...].T.astype(BF16)
    for j in range(IDX_HEADS // 2):
        r = rope_i(iq_ref[:, j * LANES:(j + 1) * LANES])
        qio_ref[:, (2 * j) * LANES:(2 * j + 1) * LANES] = r.astype(BF16)
        qio_ref[:, (2 * j + 1) * LANES:(2 * j + 2) * LANES] = pltpu.roll(r, IDX_DIM, 1).astype(BF16)
    rk = rope_i(idx_ref[...])
    kio_ref[...] = jnp.where(lane < IDX_DIM, rk, 0.0).astype(BF16)


def _dsa_prep(proj, cos_c, sin_c, cos_i, sin_i):
    T = proj.shape[0]
    tb = DSA_TK
    cb = COLBLK_MAIN

    def pspec(k):
        return pl.BlockSpec((tb, 512), lambda i, k=k: (i, cb + k))

    tspec = pl.BlockSpec((tb, LANES), lambda i: (i, 0))
    return pl.pallas_call(
        _dsa_prep_kernel,
        grid=(T // tb,),
        in_specs=[pspec(6), pspec(7), pspec(8), pspec(9),
                  pl.BlockSpec((tb, LANES), lambda i: (i, COLBLK_IDX)),
                  tspec, tspec, tspec, tspec],
        out_specs=[pl.BlockSpec((tb, 512), lambda i: (i, 0)),
                   pl.BlockSpec((tb, 512), lambda i: (i, 0)),
                   pl.BlockSpec((1, 512, tb), lambda i: (i, 0, 0)),
                   pl.BlockSpec((tb, IDX_HEADS * LANES), lambda i: (i, 0)),
                   pl.BlockSpec((tb, LANES), lambda i: (i, 0))],
        out_shape=[jax.ShapeDtypeStruct((T, 512), BF16),
                   jax.ShapeDtypeStruct((T, 512), BF16),
                   jax.ShapeDtypeStruct((T // tb, 512, tb), BF16),
                   jax.ShapeDtypeStruct((T, IDX_HEADS * LANES), BF16),
                   jax.ShapeDtypeStruct((T, LANES), BF16)],
        compiler_params=_cparams(("parallel",)),
        name="dsa_prep",
    )(proj, proj, proj, proj, proj, cos_c, sin_c, cos_i, sin_i)


DSA_TK = 512


def _sortable(score):
    score = jnp.where(score == 0.0, 0.0, score)
    bits = pltpu.bitcast(score, jnp.int32)
    return bits ^ ((bits >> 31) & 0x7FFFFFFF)


DSA_NACC = 32
I16_MIN = -32768


def _dsa_kernel(q_ref, qi_ref, idx_ref, k_ref, vt_ref, ki_ref, trit_ref, o_ref,
                qidx_scr, key_scr, hi_scr, lo_scr, *, topk):
    TK = DSA_TK
    I16 = jnp.int16
    qb = pl.program_id(1)
    nkb = (qb * Q_BLOCK + Q_BLOCK + TK - 1) // TK
    wscale = (IDX_HEADS ** -0.5) * (IDX_DIM ** -0.5)

    for h in range(IDX_HEADS):
        qidx_scr[h * Q_BLOCK:(h + 1) * Q_BLOCK, :] = qi_ref[:, h * LANES:(h + 1) * LANES]
    iwt = idx_ref[...].T
    w_rows = [iwt[IDX_DIM + h:IDX_DIM + h + 1, :] * wscale for h in range(IDX_HEADS)]
    qchunk = (qb * Q_BLOCK + lax.broadcasted_iota(jnp.int32, (1, Q_BLOCK), 1)) // CHUNK
    row_tk = lax.broadcasted_iota(jnp.int32, (TK, 1), 0)

    def p1(kb, carry):
        off = pl.multiple_of(kb * TK, TK)
        logits = _dot_t(ki_ref[pl.ds(off, TK), :], qidx_scr[...])
        s = None
        for h in range(IDX_HEADS):
            t = jnp.maximum(logits[:, h * Q_BLOCK:(h + 1) * Q_BLOCK], 0.0) * w_rows[h]
            s = t if s is None else s + t
        adm = ((off + row_tk) // CHUNK) <= qchunk
        key = _sortable(jnp.where(adm, s, -jnp.inf))
        key_scr[kb] = key
        hi_scr[kb] = (key >> 16).astype(I16)
        lo_scr[kb] = ((key & 0xFFFF) + I16_MIN).astype(I16)
        return carry

    lax.fori_loop(0, nkb, p1, 0)

    def count16(ref, cand, strict):
        c16 = cand.astype(I16)

        def body(kb, acc):
            t = ref[kb]
            for r in range(TK // DSA_NACC):
                blk = t[r * DSA_NACC:(r + 1) * DSA_NACC]
                hit = (blk > c16) if strict else (blk >= c16)
                acc = acc + jnp.where(hit, jnp.ones((), I16), jnp.zeros((), I16))
            return acc

        acc = lax.fori_loop(0, nkb, body, jnp.zeros((DSA_NACC, LANES), I16))
        return jnp.sum(acc.astype(jnp.int32), axis=0, keepdims=True)

    def bisect16(ref, kneed):
        zero = jnp.zeros((1, Q_BLOCK), jnp.int32)
        base = jnp.where(count16(ref, zero, False) >= kneed, zero, zero + I16_MIN)

        def step(i, base):
            cand = base | (jnp.int32(1) << (14 - i))
            return jnp.where(count16(ref, cand, False) >= kneed, cand, base)

        return lax.fori_loop(0, 15, step, base)

    tau_hi = bisect16(hi_scr, topk)
    cnt_gt_hi = count16(hi_scr, tau_hi, True)
    th16 = tau_hi.astype(I16)

    def keep_group(kb, carry):
        lo_scr[kb] = jnp.where(hi_scr[kb] == th16, lo_scr[kb], jnp.full((), I16_MIN, I16))
        return carry

    lax.fori_loop(0, nkb, keep_group, 0)
    tau_lo = bisect16(lo_scr, topk - cnt_gt_hi)
    cnt_gt = cnt_gt_hi + count16(lo_scr, tau_lo, True)
    tau = (tau_hi << 16) | (tau_lo - I16_MIN)
    need = (topk - cnt_gt).astype(F32)
    key_neg_inf = _sortable(jnp.full((1, 1), -jnp.inf, F32))
    tau_eff = jnp.maximum(tau, key_neg_inf + 1)

    def p2(kb, carry):
        eq_seen, ms, ls, accs = carry
        off = pl.multiple_of(kb * TK, TK)
        kt = key_scr[kb]
        eq = kt == tau
        prefix = jnp.dot(trit_ref[...], jnp.where(eq, 1.0, 0.0).astype(BF16),
                         preferred_element_type=F32)
        over = jnp.where(eq_seen + prefix > need, 1, 0)
        bias = jnp.where(jnp.where(eq, kt - over, kt) >= tau_eff, 0.0, NEG_BIG)
        eq_seen = eq_seen + prefix[TK - 1:TK, :]
        new_ms, new_ls, new_accs = [], [], []
        for h in range(C_HEADS):
            sl = slice(h * LANES, (h + 1) * LANES)
            s = _dot_t(k_ref[pl.ds(off, TK), sl], q_ref[:, sl]) + bias
            m_new = jnp.maximum(ms[h], jnp.max(s, axis=0, keepdims=True))
            alpha = jnp.exp(ms[h] - m_new)
            p = jnp.exp(s - m_new)
            new_ls.append(alpha * ls[h] + jnp.sum(p, axis=0, keepdims=True))
            new_accs.append(alpha * accs[h] + jnp.dot(vt_ref[kb, sl, :], p.astype(BF16),
                                                       preferred_element_type=F32))
            new_ms.append(m_new)
        return eq_seen, tuple(new_ms), tuple(new_ls), tuple(new_accs)

    row0 = jnp.zeros((1, Q_BLOCK), F32)
    init = (row0,
            tuple(row0 + NEG_BIG for _ in range(C_HEADS)),
            tuple(row0 for _ in range(C_HEADS)),
            tuple(jnp.zeros((LANES, Q_BLOCK), F32) for _ in range(C_HEADS)))
    _, ms, ls, accs = lax.fori_loop(0, nkb, p2, init)
    for h in range(C_HEADS):
        o_ref[:, h * LANES:(h + 1) * LANES] = (accs[h] / ls[h]).T


def _dsa(q_r, qi_r, proj, k_r, v_t, ki_r, trit, B, S):
    nq = S // Q_BLOCK
    topk = min(DSA_TOPK_MAX, S // 4)
    nkt = S // DSA_TK
    assert nkt * DSA_TK == S, "sequence length must be a multiple of the key tile"
    return pl.pallas_call(
        functools.partial(_dsa_kernel, topk=topk),
        grid=(B, nq),
        in_specs=[pl.BlockSpec((Q_BLOCK, 512), lambda b, q: (b * nq + q, 0)),
                  pl.BlockSpec((Q_BLOCK, IDX_HEADS * LANES), lambda b, q: (b * nq + q, 0)),
                  pl.BlockSpec((Q_BLOCK, LANES), lambda b, q: (b * nq + q, COLBLK_IDX)),
                  pl.BlockSpec((S, 512), lambda b, q: (b, 0)),
                  pl.BlockSpec((nkt, 512, DSA_TK), lambda b, q: (b, 0, 0)),
                  pl.BlockSpec((S, LANES), lambda b, q: (b, 0)),
                  pl.BlockSpec((DSA_TK, DSA_TK), lambda b, q: (0, 0))],
        out_specs=pl.BlockSpec((Q_BLOCK, 512), lambda b, q: (b * nq + q, 0)),
        out_shape=jax.ShapeDtypeStruct((B * S, 512), F32),
        scratch_shapes=[pltpu.VMEM((IDX_HEADS * Q_BLOCK, LANES), BF16),
                        pltpu.VMEM((nkt, DSA_TK, Q_BLOCK), jnp.int32),
                        pltpu.VMEM((nkt, DSA_TK, Q_BLOCK), jnp.int16),
                        pltpu.VMEM((nkt, DSA_TK, Q_BLOCK), jnp.int16)],
        compiler_params=_cparams(("parallel", "arbitrary"), VMEM_LIMIT),
        name="dsa",
    )(q_r, qi_r, proj, k_r, v_t, ki_r, trit)


def _merge_kernel(ya_ref, yb_ref, yc_ref, g0_ref, g1_ref, g2_ref, wb_ref, wo_ref, x_ref, o_ref):
    mixed = None
    for y_ref, g_ref, n in ((ya_ref, g0_ref, 0), (yb_ref, g1_ref, 1), (yc_ref, g2_ref, 2)):
        bp = jnp.dot(y_ref[...].astype(BF16), wb_ref[n], preferred_element_type=F32)
        t = jax.nn.sigmoid(g_ref[...]) * bp
        mixed = t if mixed is None else mixed + t
    o_ref[...] = x_ref[...] + jnp.dot(mixed.astype(BF16), wo_ref[...], preferred_element_type=F32)


def _merge(y_a, y_b, y_c, proj, w_branch, w_out, x, tb=256):
    T = x.shape[0]
    tb = min(tb, T)
    yspec = pl.BlockSpec((tb, 512), lambda i: (i, 0))

    def gspec(n):
        return pl.BlockSpec((tb, D_MODEL), lambda i, n=n: (i, n))

    return pl.pallas_call(
        _merge_kernel,
        grid=(T // tb,),
        in_specs=[yspec, yspec, yspec, gspec(0), gspec(1), gspec(2),
                  pl.BlockSpec((N_BRANCH, BRANCH_WIDTH, D_MODEL), lambda i: (0, 0, 0)),
                  pl.BlockSpec((D_MODEL, D_MODEL), lambda i: (0, 0)),
                  pl.BlockSpec((tb, D_MODEL), lambda i: (i, 0))],
        out_specs=pl.BlockSpec((tb, D_MODEL), lambda i: (i, 0)),
        out_shape=jax.ShapeDtypeStruct((T, D_MODEL), F32),
        compiler_params=_cparams(("parallel",), VMEM_LIMIT),
        name="merge",
    )(y_a, y_b, y_c, proj, proj, proj, w_branch, w_out, x)


def _oddeven_merge(lo, hi, r):
    step = r * 2
    if step < hi - lo:
        yield from _oddeven_merge(lo, hi, step)
        yield from _oddeven_merge(lo + r, hi, step)
        yield from [(i, i + r) for i in range(lo + r, hi - r, step)]
    else:
        yield (lo, lo + r)


def _oddeven_sort(lo, hi):
    if hi - lo >= 1:
        mid = lo + (hi - lo) // 2
        yield from _oddeven_sort(lo, mid)
        yield from _oddeven_sort(mid + 1, hi)
        yield from _oddeven_merge(lo, hi, 1)


SUBLANES = 8
NET16 = tuple(_oddeven_sort(0, PEER_TOPK - 1))
NET8 = tuple(_oddeven_sort(0, SUBLANES - 1))
PEER_CAND = tuple((k1, k2) for k1 in range(PEER_TOPK) for k2 in range(PEER_TOPK)
                  if (k1 + 1) * (k2 + 1) <= PEER_TOPK)


def _ce(x, i, j):
    a, b = x[i], x[j]
    x[i] = jnp.maximum(a, b)
    x[j] = jnp.minimum(a, b)


def _bitonic_sort16(m):
    d = PEER_TOPK // 2
    while d >= 1:
        for i in range(PEER_TOPK):
            if (i & d) == 0:
                _ce(m, i, i + d)
        d //= 2


def _merge_sublanes(v, shifts):
    for sh in shifts:
        w = [pltpu.roll(t, sh, 0) for t in v]
        v = [jnp.maximum(v[k], w[PEER_TOPK - 1 - k]) for k in range(PEER_TOPK)]
        _bitonic_sort16(v)
    return v


def _top16_of_rows(s):
    v = [s[SUBLANES * k:SUBLANES * (k + 1), :] for k in range(PEER_N_KEYS // SUBLANES)]
    for i, j in NET16:
        _ce(v, i, j)
    return _merge_sublanes(v, (4, 2, 1))


def _peer_prep_kernel(x_ref, g_ref, wqt_ref, sk_ref, ht_ref, r2_ref, e2_ref, c_ref, cf_ref, hb_scr):
    tb = x_ref.shape[0]
    h2 = _rms(x_ref[...], g_ref[...])
    hb_scr[...] = h2.T.astype(BF16)
    ht_ref[...] = hb_scr[...]
    sub = lax.broadcasted_iota(jnp.int32, (SUBLANES, LANES), 0)
    neg = jnp.full((SUBLANES, LANES), -jnp.inf, F32)

    def head(h, carry):
        hb = hb_scr[...]
        q1 = jnp.dot(wqt_ref[pl.ds(pl.multiple_of(h * 2 * LANES, LANES), LANES), :], hb,
                     preferred_element_type=F32)
        q2 = jnp.dot(wqt_ref[pl.ds(pl.multiple_of(h * 2 * LANES + LANES, LANES), LANES), :], hb,
                     preferred_element_type=F32)
        s1f = jnp.dot(sk_ref[2 * h], q1.astype(BF16), preferred_element_type=F32)
        s2f = jnp.dot(sk_ref[2 * h + 1], q2.astype(BF16), preferred_element_type=F32)
        for lt in range(tb // LANES):
            ls = slice(lt * LANES, (lt + 1) * LANES)
            s1 = s1f[:, ls]
            s2 = s2f[:, ls]
            a = _top16_of_rows(s1)
            b = _top16_of_rows(s2)
            packed = []
            for v in range((len(PEER_CAND) + SUBLANES - 1) // SUBLANES):
                p = neg
                for s in range(SUBLANES):
                    c = SUBLANES * v + s
                    if c < len(PEER_CAND):
                        k1, k2 = PEER_CAND[c]
                        p = jnp.where(sub == s, a[k1] + b[k2], p)
                packed.append(p)
            while len(packed) < SUBLANES:
                packed.append(neg)
            for i, j in NET8:
                _ce(packed, i, j)
            w = [pltpu.roll(t, 4, 0) for t in packed]
            m = packed + w[::-1]
            _bitonic_sort16(m)
            m = _merge_sublanes(m, (2, 1))
            thr = m[PEER_TOPK - 1][0:1, :]
            zsum = m[0] - m[0] + 1.0
            for k in range(1, PEER_TOPK):
                zsum = zsum + jnp.exp(m[k] - m[0])
            zinv = 1.0 / zsum[0:1, :]
            cnt = jnp.zeros((PEER_N_KEYS, LANES), F32)
            rank = jnp.zeros((PEER_N_KEYS, LANES), F32)
            for k in range(PEER_TOPK):
                bk = b[k][0:1, :]
                cnt = cnt + jnp.where(s1 + bk >= thr, 1.0, 0.0)
                rank = rank + jnp.where(bk > s2, 1.0, 0.0)
            r2_ref[h, :, ls] = rank.astype(r2_ref.dtype)
            e2_ref[h, :, ls] = jnp.exp(s2 - b[0][0:1, :]).astype(e2_ref.dtype)
            c_ref[h, :, ls] = cnt
            cf_ref[h, :, ls] = jnp.exp(s1 - a[0][0:1, :]) * zinv
        return carry

    lax.fori_loop(0, PEER_HEADS, head, 0)


def _peer_prep(x, gain, w_qt, sub_keys, tb=256):
    T = x.shape[0]
    tb = min(tb, T)
    aux_spec = pl.BlockSpec((PEER_HEADS, PEER_N_KEYS, tb), lambda i: (0, 0, i))
    aux_f32 = jax.ShapeDtypeStruct((PEER_HEADS, PEER_N_KEYS, T), F32)
    aux_b16 = jax.ShapeDtypeStruct((PEER_HEADS, PEER_N_KEYS, T), BF16)
    return pl.pallas_call(
        _peer_prep_kernel,
        grid=(T // tb,),
        in_specs=[pl.BlockSpec((tb, D_MODEL), lambda i: (i, 0)),
                  pl.BlockSpec((1, D_MODEL), lambda i: (0, 0)),
                  pl.BlockSpec((2 * PEER_HEADS * LANES, D_MODEL), lambda i: (0, 0)),
                  pl.BlockSpec((2 * PEER_HEADS, PEER_N_KEYS, LANES), lambda i: (0, 0, 0))],
        out_specs=[pl.BlockSpec((D_MODEL, tb), lambda i: (0, i)),
                   aux_spec, aux_spec, aux_spec, aux_spec],
        out_shape=[jax.ShapeDtypeStruct((D_MODEL, T), BF16),
                   aux_b16, aux_b16, aux_f32, aux_f32],
        scratch_shapes=[pltpu.VMEM((D_MODEL, tb), BF16)],
        compiler_params=_cparams(("parallel",), VMEM_LIMIT),
        name="peer_prep",
    )(x, gain, w_qt, sub_keys)


PEER_ET = 512


def _peer_dense_kernel(x_ref, ht_ref, r2_ref, e2_ref, c_ref, cf_ref, u_ref, vt_ref, o_ref,
                       acc_scr, ga_scr):
    et = pl.program_id(1)
    n_sub = PEER_ET // PEER_N_KEYS

    @pl.when(et == 0)
    def _():
        acc_scr[...] = jnp.zeros_like(acc_scr)

    hb = ht_ref[...]
    for half in range(n_sub):
        rs = slice(half * PEER_N_KEYS, (half + 1) * PEER_N_KEYS)
        i = et * n_sub + half
        act = jax.nn.gelu(jnp.dot(u_ref[rs, :], hb, preferred_element_type=F32))
        g = None
        for h in range(PEER_HEADS):
            c_row = c_ref[h, pl.ds(i, 1), :].astype(r2_ref.dtype)
            cf_row = cf_ref[h, pl.ds(i, 1), :].astype(e2_ref.dtype)
            t = jnp.where(r2_ref[h] < c_row, e2_ref[h] * cf_row, jnp.zeros((), e2_ref.dtype))
            g = t if g is None else g + t
        ga_scr[rs, :] = (g.astype(F32) * act).astype(BF16)
    acc_scr[...] += jnp.dot(vt_ref[...], ga_scr[...], preferred_element_type=F32)

    @pl.when(et == pl.num_programs(1) - 1)
    def _():
        o_ref[...] = x_ref[...] + acc_scr[...].T


def _peer_dense(x, ht, r2, e2, cnt, coef, u_tab, vt_tab, tb=512):
    T = x.shape[0]
    tb = min(tb, T)
    n_exp = u_tab.shape[0]
    aux_spec = pl.BlockSpec((PEER_HEADS, PEER_N_KEYS, tb), lambda i, e: (0, 0, i))
    return pl.pallas_call(
        _peer_dense_kernel,
        grid=(T // tb, n_exp // PEER_ET),
        in_specs=[pl.BlockSpec((tb, D_MODEL), lambda i, e: (i, 0)),
                  pl.BlockSpec((D_MODEL, tb), lambda i, e: (0, i)),
                  aux_spec, aux_spec, aux_spec, aux_spec,
                  pl.BlockSpec((PEER_ET, D_MODEL), lambda i, e: (e, 0)),
                  pl.BlockSpec((D_MODEL, PEER_ET), lambda i, e: (0, e))],
        out_specs=pl.BlockSpec((tb, D_MODEL), lambda i, e: (i, 0)),
        out_shape=jax.ShapeDtypeStruct((T, D_MODEL), F32),
        scratch_shapes=[pltpu.VMEM((D_MODEL, tb), F32),
                        pltpu.VMEM((PEER_ET, tb), BF16)],
        compiler_params=_cparams(("parallel", "arbitrary"), VMEM_LIMIT),
        name="peer_dense",
    )(x, ht, r2, e2, cnt, coef, u_tab, vt_tab)


def _final_kernel(x_ref, g_ref, o_ref):
    o_ref[...] = _rms(x_ref[...], g_ref[...])


def _final_norm(x, gain, tb=1024):
    T = x.shape[0]
    tb = min(tb, T)
    return pl.pallas_call(
        _final_kernel,
        grid=(T // tb,),
        in_specs=[pl.BlockSpec((tb, D_MODEL), lambda i: (i, 0)),
                  pl.BlockSpec((1, D_MODEL), lambda i: (0, 0))],
        out_specs=pl.BlockSpec((tb, D_MODEL), lambda i: (i, 0)),
        out_shape=jax.ShapeDtypeStruct((T, D_MODEL), F32),
        compiler_params=_cparams(("parallel",)),
        name="final_norm",
    )(x, gain)


def _rope_tables(positions):
    pos = positions.astype(F32).reshape(-1, 1)

    def tables(dim):
        half = dim // 2
        inv = ROPE_THETA ** (-jnp.arange(half, dtype=F32) / half)
        ang = pos * inv
        cos = jnp.cos(ang)
        sin = jnp.sin(ang)
        reps = LANES // dim
        return (jnp.tile(jnp.concatenate([cos, cos], axis=-1), (1, reps)),
                jnp.tile(jnp.concatenate([-sin, sin], axis=-1), (1, reps)))

    cos_c, sin_c = tables(C_HEAD_DIM)
    cos_i, sin_i = tables(IDX_DIM)
    return cos_c, sin_c, cos_i, sin_i


def _pack_w_in(w):
    n_idx = IDX_DIM + IDX_HEADS
    main = w[:, :N_MAIN]
    idx = w[:, N_MAIN:N_MAIN + n_idx]
    gates = w[:, N_MAIN + n_idx:]
    pad = jnp.zeros((w.shape[0], LANES - n_idx), w.dtype)
    return jnp.concatenate([gates, main, idx, pad], axis=1).astype(BF16)


def _layer(x, l, B, S, tabs, tri, lbp_all, p):
    proj = _inproj(x, p["norm1_gain"][l][None, :], _pack_w_in(p["w_in"][l]))
    y_a = _hgrn(proj, lbp_all[l], p["hgrn_norm_gain"][l][None, :], B, S)
    bias_full = jnp.repeat(p["gmlp_b_s"][l].T, B_BLOCK, axis=1)
    y_b = _gmlp(proj, p["gmlp_norm_gain"][l][None, :], p["gmlp_w_s"][l], bias_full)
    q_r, k_r, v_t, qi_r, ki_r = _dsa_prep(proj, *tabs)
    y_c = _dsa(q_r, qi_r, proj, k_r, v_t, ki_r, tri, B, S)
    x = _merge(y_a, y_b, y_c, proj, p["w_branch"][l].astype(BF16), p["w_out"][l].astype(BF16), x)
    sk = p["peer_sub_keys"][l].reshape(2 * PEER_HEADS, PEER_N_KEYS, LANES).astype(BF16)
    ht, r2, e2, cnt, coef = _peer_prep(x, p["norm2_gain"][l][None, :],
                                       p["peer_w_q"][l].T.astype(BF16), sk)
    return _peer_dense(x, ht, r2, e2, cnt, coef, p["peer_u"][l].astype(BF16),
                       p["peer_v"][l].T.astype(BF16))


def kernel(x, positions, norm1_gain, w_in, hgrn_lb_logits, hgrn_norm_gain, gmlp_norm_gain,
           gmlp_w_s, gmlp_b_s, w_branch, w_out, norm2_gain, peer_w_q, peer_sub_keys,
           peer_u, peer_v, final_gain):
    B, S, _ = x.shape
    depth = w_in.shape[0]
    p = dict(norm1_gain=norm1_gain, w_in=w_in, hgrn_norm_gain=hgrn_norm_gain,
             gmlp_norm_gain=gmlp_norm_gain, gmlp_w_s=gmlp_w_s, gmlp_b_s=gmlp_b_s,
             w_branch=w_branch, w_out=w_out, norm2_gain=norm2_gain, peer_w_q=peer_w_q,
             peer_sub_keys=peer_sub_keys, peer_u=peer_u, peer_v=peer_v)
    sm = jax.nn.softmax(hgrn_lb_logits.astype(F32), axis=0)
    cs = jnp.cumsum(sm, axis=0)
    lb = cs - cs[0:1]
    lbp_all = jnp.stack([jnp.log(lb), jnp.log1p(-lb), 1.0 - lb], axis=1)
    tabs = _rope_tables(positions)
    r = lax.broadcasted_iota(jnp.int32, (DSA_TK, DSA_TK), 0)
    c = lax.broadcasted_iota(jnp.int32, (DSA_TK, DSA_TK), 1)
    tri = (r >= c).astype(BF16)
    xt = x.reshape(B * S, D_MODEL)
    for l in range(depth):
        xt = _layer(xt, l, B, S, tabs, tri, lbp_all, p)
    return _final_norm(xt, final_gain[None, :]).reshape(B, S, D_MODEL)
```

```python
import functools

import numpy as np
import jax
import jax.numpy as jnp
from jax import lax
from jax.experimental import pallas as pl
from jax.experimental.pallas import tpu as pltpu

F32 = jnp.float32
BF16 = jnp.bfloat16

D_MODEL = 1024
CHUNK = 64
EPS = 1e-6
ROPE_THETA = 10000.0
A_HEADS = 4
A_DK = 128
B_GROUPS = 4
B_BLOCK = 128
C_HEADS = 4
C_HEAD_DIM = 128
IDX_HEADS = 8
IDX_DIM = 64
DSA_TOPK_MAX = 256
Q_BLOCK = 128
N_BRANCH = 3
BRANCH_WIDTH = 512
PEER_HEADS = 8
PEER_N_KEYS = 128
PEER_TOPK = 16

LANES = 128
VMEM_LIMIT = 56 * 1024 * 1024

N_MAIN = 10 * 512
N_PACKED = N_BRANCH * D_MODEL + N_MAIN + LANES
COLBLK_MAIN = (N_BRANCH * D_MODEL) // 512
COLBLK_IDX = (N_BRANCH * D_MODEL + N_MAIN) // LANES

NEG_BIG = -1e30
INT_MIN = -2147483648


def _cparams(sem, vmem=None):
    return pltpu.CompilerParams(dimension_semantics=sem, vmem_limit_bytes=vmem)


def _rms(x, gain):
    return x * lax.rsqrt(jnp.mean(x * x, axis=-1, keepdims=True) + EPS) * gain


def _dot_t(a, b):
    return lax.dot_general(a, b, (((1,), (1,)), ((), ())), preferred_element_type=F32)


def _inproj_kernel(x_ref, g_ref, w_ref, o_ref, h_scr):
    @pl.when(pl.program_id(1) == 0)
    def _():
        h_scr[...] = _rms(x_ref[...], g_ref[...]).astype(BF16)

    o_ref[...] = jnp.dot(h_scr[...], w_ref[...], preferred_element_type=F32)


def _inproj(x, gain, w_packed, tm=1024, tn=640):
    T = x.shape[0]
    tm = min(tm, T)
    return pl.pallas_call(
        _inproj_kernel,
        grid=(T // tm, N_PACKED // tn),
        in_specs=[pl.BlockSpec((tm, D_MODEL), lambda i, j: (i, 0)),
                  pl.BlockSpec((1, D_MODEL), lambda i, j: (0, 0)),
                  pl.BlockSpec((D_MODEL, tn), lambda i, j: (0, j))],
        out_specs=pl.BlockSpec((tm, tn), lambda i, j: (i, j)),
        out_shape=jax.ShapeDtypeStruct((T, N_PACKED), F32),
        scratch_shapes=[pltpu.VMEM((tm, D_MODEL), BF16)],
        compiler_params=_cparams(("parallel", "arbitrary"), VMEM_LIMIT),
        name="inproj",
    )(x, gain, w_packed)


HG_C = 128
HG_SUB = 16


def _hgrn_kernel(q_ref, f_ref, i_ref, g_ref, lbp_ref, gain_ref, o_ref, st_scr):
    @pl.when(pl.program_id(1) == 0)
    def _():
        st_scr[...] = jnp.zeros_like(st_scr)

    C = HG_C
    row = lax.broadcasted_iota(jnp.int32, (C, 1), 0)
    rowm = lax.broadcasted_iota(jnp.int32, (C, C), 0)
    colm = lax.broadcasted_iota(jnp.int32, (C, C), 1)
    gain = gain_ref[...]

    for h in range(A_HEADS):
        sl = slice(h * A_DK, (h + 1) * A_DK)
        z = f_ref[:, sl]
        loglb = lbp_ref[0:1, sl]
        log1mlb = lbp_ref[1:2, sl]
        omlb = lbp_ref[2:3, sl]
        lsig = -(jnp.maximum(-z, 0.0) + jnp.log1p(jnp.exp(-jnp.abs(z))))
        cterm = log1mlb + lsig
        logf = jnp.maximum(loglb, cterm) + jnp.log1p(jnp.exp(-jnp.abs(loglb - cterm)))
        key = omlb * jax.nn.sigmoid(-z)
        qf = jax.nn.silu(q_ref[:, sl]) * (A_DK ** -0.5)
        vf = i_ref[:, sl]

        b = logf
        sh = 1
        while sh < C:
            b = b + jnp.where(row >= sh, pltpu.roll(b, sh, 0), 0.0)
            sh *= 2

        attn = jnp.zeros((C, C), F32)
        m = C // 2
        while m >= HG_SUB:
            ref = jnp.concatenate(
                [jnp.broadcast_to(b[j * 2 * m + m - 1:j * 2 * m + m, :], (2 * m, A_DK))
                 for j in range(C // (2 * m))], axis=0)
            qt = qf * jnp.exp(jnp.minimum(b - ref, 0.0))
            kt = key * jnp.exp(jnp.minimum(ref - b, 0.0))
            a = _dot_t(qt.astype(BF16), kt.astype(BF16))
            msk = ((rowm // (2 * m)) == (colm // (2 * m))) & ((rowm % (2 * m)) >= m) & ((colm % (2 * m)) < m)
            attn = attn + jnp.where(msk, a, 0.0)
            m //= 2

        nsub = C // HG_SUB
        for j in range(HG_SUB):
            kj = jnp.concatenate(
                [jnp.broadcast_to(key[i * HG_SUB + j:i * HG_SUB + j + 1, :], (HG_SUB, A_DK))
                 for i in range(nsub)], axis=0)
            bj = jnp.concatenate(
                [jnp.broadcast_to(b[i * HG_SUB + j:i * HG_SUB + j + 1, :], (HG_SUB, A_DK))
                 for i in range(nsub)], axis=0)
            d = jnp.exp(jnp.minimum(b - bj, 0.0)) * qf * kj
            red = jnp.sum(d, axis=-1, keepdims=True)
            hit = (colm == (rowm // HG_SUB) * HG_SUB + j) & ((rowm % HG_SUB) >= j)
            attn = jnp.where(hit, red, attn)

        st = st_scr[h]
        o = jnp.dot(attn.astype(BF16), vf.astype(BF16), preferred_element_type=F32)
        o = o + _dot_t((qf * jnp.exp(b)).astype(BF16), st.astype(BF16))
        blast = b[C - 1:C, :]
        kdec = key * jnp.exp(blast - b)
        upd = lax.dot_general(vf.astype(BF16), kdec.astype(BF16), (((0,), (0,)), ((), ())),
                              preferred_element_type=F32)
        st_scr[h] = st * jnp.exp(blast) + upd

        o = _rms(o, gain)
        o_ref[:, sl] = o * jax.nn.silu(g_ref[:, sl])


def _hgrn(proj, lbp, gain, B, S):
    nc = S // HG_C
    cb = COLBLK_MAIN

    def spec(k):
        return pl.BlockSpec((HG_C, 512), lambda b, c, k=k: (b * nc + c, cb + k))

    return pl.pallas_call(
        _hgrn_kernel,
        grid=(B, nc),
        in_specs=[spec(0), spec(1), spec(2), spec(3),
                  pl.BlockSpec((3, 512), lambda b, c: (0, 0)),
                  pl.BlockSpec((1, A_DK), lambda b, c: (0, 0))],
        out_specs=pl.BlockSpec((HG_C, 512), lambda b, c: (b * nc + c, 0)),
        out_shape=jax.ShapeDtypeStruct((B * S, 512), F32),
        scratch_shapes=[pltpu.VMEM((A_HEADS, A_DK, A_DK), F32)],
        compiler_params=_cparams(("parallel", "arbitrary")),
        name="hgrn",
    )(proj, proj, proj, proj, lbp, gain)


def _gmlp_kernel(u_ref, v_ref, g_ref, w_ref, b_ref, o_ref, *, nsub):
    r = lax.broadcasted_iota(jnp.int32, (B_BLOCK, B_BLOCK), 0)
    c = lax.broadcasted_iota(jnp.int32, (B_BLOCK, B_BLOCK), 1)
    tril = r >= c
    u = jax.nn.gelu(u_ref[...])
    v = _rms(jax.nn.gelu(v_ref[...]), g_ref[...]).astype(BF16)
    bias = b_ref[...]
    for g in range(B_GROUPS):
        w = jnp.where(tril, w_ref[g], 0.0).astype(BF16)
        cs = slice(g * LANES, (g + 1) * LANES)
        for s in range(nsub):
            rs = slice(s * B_BLOCK, (s + 1) * B_BLOCK)
            mixed = jnp.dot(w, v[rs, cs], preferred_element_type=F32) + bias[:, cs]
            o_ref[rs, cs] = u[rs, cs] * mixed


def _gmlp(proj, gain, w_s, bias_full, tb=512):
    T = proj.shape[0]
    tb = min(tb, T)
    cb = COLBLK_MAIN
    return pl.pallas_call(
        functools.partial(_gmlp_kernel, nsub=tb // B_BLOCK),
        grid=(T // tb,),
        in_specs=[pl.BlockSpec((tb, 512), lambda i: (i, cb + 4)),
                  pl.BlockSpec((tb, 512), lambda i: (i, cb + 5)),
                  pl.BlockSpec((1, 512), lambda i: (0, 0)),
                  pl.BlockSpec((B_GROUPS, B_BLOCK, B_BLOCK), lambda i: (0, 0, 0)),
                  pl.BlockSpec((B_BLOCK, 512), lambda i: (0, 0))],
        out_specs=pl.BlockSpec((tb, 512), lambda i: (i, 0)),
        out_shape=jax.ShapeDtypeStruct((T, 512), F32),
        compiler_params=_cparams(("parallel",)),
        name="gmlp",
    )(proj, proj, gain, w_s, bias_full)


def _dsa_prep_kernel(q_ref, k_ref, v_ref, iq_ref, idx_ref, cc_ref, sc_ref, ci_ref, si_ref,
                     qo_ref, ko_ref, vo_ref, qio_ref, kio_ref):
    cc = cc_ref[...]
    sc = sc_ref[...]
    ci = ci_ref[...]
    si = si_ref[...]
    lane = lax.broadcasted_iota(jnp.int32, ci.shape, 1)
    first_half = (lane % IDX_DIM) < (IDX_DIM // 2)

    def rope_i(x):
        partner = jnp.where(first_half, pltpu.roll(x, LANES - IDX_DIM // 2, 1),
                            pltpu.roll(x, IDX_DIM // 2, 1))
        return x * ci + partner * si

    for h in range(C_HEADS):
        sl = slice(h * LANES, (h + 1) * LANES)
        x = q_ref[:, sl]
        qo_ref[:, sl] = ((x * cc + pltpu.roll(x, C_HEAD_DIM // 2, 1) * sc)
                         * (C_HEAD_DIM ** -0.5)).astype(BF16)
        x = k_ref[:, sl]
        ko_ref[:, sl] = (x * cc + pltpu.roll(x, C_HEAD_DIM // 2, 1) * sc).astype(BF16)
    vo_ref[0] = v_ref[...].T.astype(BF16)
    for j in range(IDX_HEADS // 2):
        r = rope_i(iq_ref[:, j * LANES:(j + 1) * LANES])
        qio_ref[:, (2 * j) * LANES:(2 * j + 1) * LANES] = r.astype(BF16)
        qio_ref[:, (2 * j + 1) * LANES:(2 * j + 2) * LANES] = pltpu.roll(r, IDX_DIM, 1).astype(BF16)
    rk = rope_i(idx_ref[...])
    kio_ref[...] = jnp.where(lane < IDX_DIM, rk, 0.0).astype(BF16)


def _dsa_prep(proj, cos_c, sin_c, cos_i, sin_i):
    T = proj.shape[0]
    tb = DSA_TK
    cb = COLBLK_MAIN

    def pspec(k):
        return pl.BlockSpec((tb, 512), lambda i, k=k: (i, cb + k))

    tspec = pl.BlockSpec((tb, LANES), lambda i: (i, 0))
    return pl.pallas_call(
        _dsa_prep_kernel,
        grid=(T // tb,),
        in_specs=[pspec(6), pspec(7), pspec(8), pspec(9),
                  pl.BlockSpec((tb, LANES), lambda i: (i, COLBLK_IDX)),
                  tspec, tspec, tspec, tspec],
        out_specs=[pl.BlockSpec((tb, 512), lambda i: (i, 0)),
                   pl.BlockSpec((tb, 512), lambda i: (i, 0)),
                   pl.BlockSpec((1, 512, tb), lambda i: (i, 0, 0)),
                   pl.BlockSpec((tb, IDX_HEADS * LANES), lambda i: (i, 0)),
                   pl.BlockSpec((tb, LANES), lambda i: (i, 0))],
        out_shape=[jax.ShapeDtypeStruct((T, 512), BF16),
                   jax.ShapeDtypeStruct((T, 512), BF16),
                   jax.ShapeDtypeStruct((T // tb, 512, tb), BF16),
                   jax.ShapeDtypeStruct((T, IDX_HEADS * LANES), BF16),
                   jax.ShapeDtypeStruct((T, LANES), BF16)],
        compiler_params=_cparams(("parallel",)),
        name="dsa_prep",
    )(proj, proj, proj, proj, proj, cos_c, sin_c, cos_i, sin_i)


DSA_TK = 512


def _sortable(score):
    score = jnp.where(score == 0.0, 0.0, score)
    bits = pltpu.bitcast(score, jnp.int32)
    return bits ^ ((bits >> 31) & 0x7FFFFFFF)


DSA_NACC = 32
I16_MIN = -32768


def _dsa_kernel(q_ref, qi_ref, idx_ref, k_ref, vt_ref, ki_ref, trit_ref, o_ref,
                qidx_scr, key_scr, hi_scr, lo_scr, acc_scr, *, topk):
    TK = DSA_TK
    I16 = jnp.int16
    qb = pl.program_id(1)
    nkb = (qb * Q_BLOCK + Q_BLOCK + TK - 1) // TK
    wscale = (IDX_HEADS ** -0.5) * (IDX_DIM ** -0.5)

    for h in range(IDX_HEADS):
        qidx_scr[h * Q_BLOCK:(h + 1) * Q_BLOCK, :] = qi_ref[:, h * LANES:(h + 1) * LANES]
    iwt = idx_ref[...].T
    w_rows = [iwt[IDX_DIM + h:IDX_DIM + h + 1, :] * wscale for h in range(IDX_HEADS)]
    qchunk = (qb * Q_BLOCK + lax.broadcasted_iota(jnp.int32, (1, Q_BLOCK), 1)) // CHUNK
    row_tk = lax.broadcasted_iota(jnp.int32, (TK, 1), 0)

    def p1(kb, carry):
        off = pl.multiple_of(kb * TK, TK)
        logits = _dot_t(ki_ref[pl.ds(off, TK), :], qidx_scr[...])
        s = None
        for h in range(IDX_HEADS):
            t = jnp.maximum(logits[:, h * Q_BLOCK:(h + 1) * Q_BLOCK], 0.0) * w_rows[h]
            s = t if s is None else s + t
        adm = ((off + row_tk) // CHUNK) <= qchunk
        key = _sortable(jnp.where(adm, s, -jnp.inf))
        key_scr[kb] = key
        hi_scr[kb] = (key >> 16).astype(I16)
        lo_scr[kb] = ((key & 0xFFFF) + I16_MIN).astype(I16)
        return carry

    lax.fori_loop(0, nkb, p1, 0)

    def count16(ref, cand, strict):
        c16 = cand.astype(I16)

        def body(kb, acc):
            t = ref[kb]
            for r in range(TK // DSA_NACC):
                blk = t[r * DSA_NACC:(r + 1) * DSA_NACC]
                hit = (blk > c16) if strict else (blk >= c16)
                acc = acc + jnp.where(hit, jnp.ones((), I16), jnp.zeros((), I16))
            return acc

        acc = lax.fori_loop(0, nkb, body, jnp.zeros((DSA_NACC, LANES), I16))
        return jnp.sum(acc.astype(jnp.int32), axis=0, keepdims=True)

    def bisect16(ref, kneed):
        zero = jnp.zeros((1, Q_BLOCK), jnp.int32)
        base = jnp.where(count16(ref, zero, False) >= kneed, zero, zero + I16_MIN)

        def step(i, base):
            cand = base | (jnp.int32(1) << (14 - i))
            return jnp.where(count16(ref, cand, False) >= kneed, cand, base)

        return lax.fori_loop(0, 15, step, base)

    tau_hi = bisect16(hi_scr, topk)
    cnt_gt_hi = count16(hi_scr, tau_hi, True)
    th16 = tau_hi.astype(I16)

    def keep_group(kb, carry):
        lo_scr[kb] = jnp.where(hi_scr[kb] == th16, lo_scr[kb], jnp.full((), I16_MIN, I16))
        return carry

    lax.fori_loop(0, nkb, keep_group, 0)
    tau_lo = bisect16(lo_scr, topk - cnt_gt_hi)
    cnt_gt = cnt_gt_hi + count16(lo_scr, tau_lo, True)
    tau = (tau_hi << 16) | (tau_lo - I16_MIN)
    need = (topk - cnt_gt).astype(F32)
    key_neg_inf = _sortable(jnp.full((1, 1), -jnp.inf, F32))
    tau_eff = jnp.maximum(tau, key_neg_inf + 1)

    acc_scr[...] = jnp.zeros_like(acc_scr)
    hsl = [slice(h * LANES, (h + 1) * LANES) for h in range(C_HEADS)]

    def p2(kb, carry, tie_order):
        eq_seen, ms, ls = carry
        off = pl.multiple_of(kb * TK, TK)
        kt = key_scr[kb]
        if tie_order:
            eq = kt == tau
            prefix = jnp.dot(trit_ref[...], jnp.where(eq, 1.0, 0.0).astype(BF16),
                             preferred_element_type=F32)
        ss = [_dot_t(k_ref[pl.ds(off, TK), sl], q_ref[:, sl]) for sl in hsl]
        if tie_order:
            over = jnp.where(eq_seen + prefix > need, 1, 0)
            kt = jnp.where(eq, kt - over, kt)
            eq_seen = eq_seen + prefix[TK - 1:TK, :]
        bias = jnp.where(kt >= tau_eff, 0.0, NEG_BIG)
        new_ms, new_ls, alphas, ps = [], [], [], []
        for h in range(C_HEADS):
            s = ss[h] + bias
            m_new = jnp.maximum(ms[h], jnp.max(s, axis=0, keepdims=True))
            alpha = jnp.exp(ms[h] - m_new)
            p = jnp.exp(s - m_new)
            new_ls.append(alpha * ls[h] + jnp.sum(p, axis=0, keepdims=True))
            new_ms.append(m_new)
            alphas.append(alpha)
            ps.append(p.astype(BF16))
        for h in range(C_HEADS):
            acc_scr[h] = alphas[h] * acc_scr[h] + jnp.dot(vt_ref[kb, hsl[h], :], ps[h],
                                                          preferred_element_type=F32)
        return eq_seen, tuple(new_ms), tuple(new_ls)

    row0 = jnp.zeros((1, Q_BLOCK), F32)
    init = (row0,
            tuple(row0 + NEG_BIG for _ in range(C_HEADS)),
            tuple(row0 for _ in range(C_HEADS)))
    cnt_ge = cnt_gt_hi + count16(lo_scr, tau_lo, False)
    _, ms, ls = lax.cond(
        jnp.max(cnt_ge) > topk,
        lambda: lax.fori_loop(0, nkb, functools.partial(p2, tie_order=True), init),
        lambda: lax.fori_loop(0, nkb, functools.partial(p2, tie_order=False), init))
    for h in range(C_HEADS):
        o_ref[:, h * LANES:(h + 1) * LANES] = (acc_scr[h] / ls[h]).T


def _dsa(q_r, qi_r, proj, k_r, v_t, ki_r, trit, B, S):
    nq = S // Q_BLOCK
    topk = min(DSA_TOPK_MAX, S // 4)
    nkt = S // DSA_TK
    assert nkt * DSA_TK == S, "sequence length must be a multiple of the key tile"
    return pl.pallas_call(
        functools.partial(_dsa_kernel, topk=topk),
        grid=(B, nq),
        in_specs=[pl.BlockSpec((Q_BLOCK, 512), lambda b, q: (b * nq + q, 0)),
                  pl.BlockSpec((Q_BLOCK, IDX_HEADS * LANES), lambda b, q: (b * nq + q, 0)),
                  pl.BlockSpec((Q_BLOCK, LANES), lambda b, q: (b * nq + q, COLBLK_IDX)),
                  pl.BlockSpec((S, 512), lambda b, q: (b, 0)),
                  pl.BlockSpec((nkt, 512, DSA_TK), lambda b, q: (b, 0, 0)),
                  pl.BlockSpec((S, LANES), lambda b, q: (b, 0)),
                  pl.BlockSpec((DSA_TK, DSA_TK), lambda b, q: (0, 0))],
        out_specs=pl.BlockSpec((Q_BLOCK, 512), lambda b, q: (b * nq + q, 0)),
        out_shape=jax.ShapeDtypeStruct((B * S, 512), F32),
        scratch_shapes=[pltpu.VMEM((IDX_HEADS * Q_BLOCK, LANES), BF16),
                        pltpu.VMEM((nkt, DSA_TK, Q_BLOCK), jnp.int32),
                        pltpu.VMEM((nkt, DSA_TK, Q_BLOCK), jnp.int16),
                        pltpu.VMEM((nkt, DSA_TK, Q_BLOCK), jnp.int16),
                        pltpu.VMEM((C_HEADS, LANES, Q_BLOCK), F32)],
        compiler_params=_cparams(("parallel", "arbitrary"), VMEM_LIMIT),
        name="dsa",
    )(q_r, qi_r, proj, k_r, v_t, ki_r, trit)


def _merge_kernel(ya_ref, yb_ref, yc_ref, g0_ref, g1_ref, g2_ref, wb_ref, wo_ref, x_ref, o_ref):
    mixed = None
    for y_ref, g_ref, n in ((ya_ref, g0_ref, 0), (yb_ref, g1_ref, 1), (yc_ref, g2_ref, 2)):
        bp = jnp.dot(y_ref[...].astype(BF16), wb_ref[n], preferred_element_type=F32)
        t = jax.nn.sigmoid(g_ref[...]) * bp
        mixed = t if mixed is None else mixed + t
    o_ref[...] = x_ref[...] + jnp.dot(mixed.astype(BF16), wo_ref[...], preferred_element_type=F32)


def _merge(y_a, y_b, y_c, proj, w_branch, w_out, x, tb=256):
    T = x.shape[0]
    tb = min(tb, T)
    yspec = pl.BlockSpec((tb, 512), lambda i: (i, 0))

    def gspec(n):
        return pl.BlockSpec((tb, D_MODEL), lambda i, n=n: (i, n))

    return pl.pallas_call(
        _merge_kernel,
        grid=(T // tb,),
        in_specs=[yspec, yspec, yspec, gspec(0), gspec(1), gspec(2),
                  pl.BlockSpec((N_BRANCH, BRANCH_WIDTH, D_MODEL), lambda i: (0, 0, 0)),
                  pl.BlockSpec((D_MODEL, D_MODEL), lambda i: (0, 0)),
                  pl.BlockSpec((tb, D_MODEL), lambda i: (i, 0))],
        out_specs=pl.BlockSpec((tb, D_MODEL), lambda i: (i, 0)),
        out_shape=jax.ShapeDtypeStruct((T, D_MODEL), F32),
        compiler_params=_cparams(("parallel",), VMEM_LIMIT),
        name="merge",
    )(y_a, y_b, y_c, proj, proj, proj, w_branch, w_out, x)


def _oddeven_merge(lo, hi, r):
    step = r * 2
    if step < hi - lo:
        yield from _oddeven_merge(lo, hi, step)
        yield from _oddeven_merge(lo + r, hi, step)
        yield from [(i, i + r) for i in range(lo + r, hi - r, step)]
    else:
        yield (lo, lo + r)


def _oddeven_sort(lo, hi):
    if hi - lo >= 1:
        mid = lo + (hi - lo) // 2
        yield from _oddeven_sort(lo, mid)
        yield from _oddeven_sort(mid + 1, hi)
        yield from _oddeven_merge(lo, hi, 1)


SUBLANES = 8
NET16 = tuple(_oddeven_sort(0, PEER_TOPK - 1))
NET8 = tuple(_oddeven_sort(0, SUBLANES - 1))
PEER_CAND = tuple((k1, k2) for k1 in range(PEER_TOPK) for k2 in range(PEER_TOPK)
                  if (k1 + 1) * (k2 + 1) <= PEER_TOPK)


def _ce(x, i, j):
    a, b = x[i], x[j]
    x[i] = jnp.maximum(a, b)
    x[j] = jnp.minimum(a, b)


def _bitonic_sort16(m):
    d = PEER_TOPK // 2
    while d >= 1:
        for i in range(PEER_TOPK):
            if (i & d) == 0:
                _ce(m, i, i + d)
        d //= 2


def _merge_sublanes(v, shifts):
    for sh in shifts:
        w = [pltpu.roll(t, sh, 0) for t in v]
        v = [jnp.maximum(v[k], w[PEER_TOPK - 1 - k]) for k in range(PEER_TOPK)]
        _bitonic_sort16(v)
    return v


def _top16_of_rows(s):
    v = [s[SUBLANES * k:SUBLANES * (k + 1), :] for k in range(PEER_N_KEYS // SUBLANES)]
    for i, j in NET16:
        _ce(v, i, j)
    return _merge_sublanes(v, (4, 2, 1))


def _peer_prep_kernel(x_ref, g_ref, wqt_ref, sk_ref, ht_ref, r2_ref, e2_ref, c_ref, cf_ref, hb_scr):
    tb = x_ref.shape[0]
    h2 = _rms(x_ref[...], g_ref[...])
    hb_scr[...] = h2.T.astype(BF16)
    ht_ref[...] = hb_scr[...]
    sub = lax.broadcasted_iota(jnp.int32, (SUBLANES, LANES), 0)
    neg = jnp.full((SUBLANES, LANES), -jnp.inf, F32)

    def head(h, carry):
        hb = hb_scr[...]
        q1 = jnp.dot(wqt_ref[pl.ds(pl.multiple_of(h * 2 * LANES, LANES), LANES), :], hb,
                     preferred_element_type=F32)
        q2 = jnp.dot(wqt_ref[pl.ds(pl.multiple_of(h * 2 * LANES + LANES, LANES), LANES), :], hb,
                     preferred_element_type=F32)
        s1f = jnp.dot(sk_ref[2 * h], q1.astype(BF16), preferred_element_type=F32)
        s2f = jnp.dot(sk_ref[2 * h + 1], q2.astype(BF16), preferred_element_type=F32)
        for lt in range(tb // LANES):
            ls = slice(lt * LANES, (lt + 1) * LANES)
            s1 = s1f[:, ls]
            s2 = s2f[:, ls]
            a = _top16_of_rows(s1)
            b = _top16_of_rows(s2)
            packed = []
            for v in range((len(PEER_CAND) + SUBLANES - 1) // SUBLANES):
                p = neg
                for s in range(SUBLANES):
                    c = SUBLANES * v + s
                    if c < len(PEER_CAND):
                        k1, k2 = PEER_CAND[c]
                        p = jnp.where(sub == s, a[k1] + b[k2], p)
                packed.append(p)
            while len(packed) < SUBLANES:
                packed.append(neg)
            for i, j in NET8:
                _ce(packed, i, j)
            w = [pltpu.roll(t, 4, 0) for t in packed]
            m = packed + w[::-1]
            _bitonic_sort16(m)
            m = _merge_sublanes(m, (2, 1))
            thr = m[PEER_TOPK - 1][0:1, :]
            zsum = m[0] - m[0] + 1.0
            for k in range(1, PEER_TOPK):
                zsum = zsum + jnp.exp(m[k] - m[0])
            zinv = 1.0 / zsum[0:1, :]
            cnt = jnp.zeros((PEER_N_KEYS, LANES), F32)
            rank = jnp.zeros((PEER_N_KEYS, LANES), F32)
            for k in range(PEER_TOPK):
                bk = b[k][0:1, :]
                cnt = cnt + jnp.where(s1 + bk >= thr, 1.0, 0.0)
                rank = rank + jnp.where(bk > s2, 1.0, 0.0)
            r2_ref[h, :, ls] = rank.astype(r2_ref.dtype)
            e2_ref[h, :, ls] = jnp.exp(s2 - b[0][0:1, :]).astype(e2_ref.dtype)
            c_ref[h, :, ls] = cnt
            cf_ref[h, :, ls] = jnp.exp(s1 - a[0][0:1, :]) * zinv
        return carry

    lax.fori_loop(0, PEER_HEADS, head, 0)


def _peer_prep(x, gain, w_qt, sub_keys, tb=256):
    T = x.shape[0]
    tb = min(tb, T)
    aux_spec = pl.BlockSpec((PEER_HEADS, PEER_N_KEYS, tb), lambda i: (0, 0, i))
    aux_f32 = jax.ShapeDtypeStruct((PEER_HEADS, PEER_N_KEYS, T), F32)
    aux_b16 = jax.ShapeDtypeStruct((PEER_HEADS, PEER_N_KEYS, T), BF16)
    return pl.pallas_call(
        _peer_prep_kernel,
        grid=(T // tb,),
        in_specs=[pl.BlockSpec((tb, D_MODEL), lambda i: (i, 0)),
                  pl.BlockSpec((1, D_MODEL), lambda i: (0, 0)),
                  pl.BlockSpec((2 * PEER_HEADS * LANES, D_MODEL), lambda i: (0, 0)),
                  pl.BlockSpec((2 * PEER_HEADS, PEER_N_KEYS, LANES), lambda i: (0, 0, 0))],
        out_specs=[pl.BlockSpec((D_MODEL, tb), lambda i: (0, i)),
                   aux_spec, aux_spec, aux_spec, aux_spec],
        out_shape=[jax.ShapeDtypeStruct((D_MODEL, T), BF16),
                   aux_b16, aux_b16, aux_f32, aux_f32],
        scratch_shapes=[pltpu.VMEM((D_MODEL, tb), BF16)],
        compiler_params=_cparams(("parallel",), VMEM_LIMIT),
        name="peer_prep",
    )(x, gain, w_qt, sub_keys)


PEER_ET = 1024
PEER_MM = 256


GELU_K1 = -2.0 * float(np.sqrt(2.0 / np.pi)) * float(np.log2(np.e))
GELU_K3 = GELU_K1 * 0.044715


def _peer_dense_kernel(x_ref, ht_ref, r2_ref, e2_ref, c_ref, cf_ref, u_ref, vt_ref, o_ref,
                       acc_scr, a_scr, ga_scr):
    et = pl.program_id(1)
    n_sub = PEER_ET // PEER_N_KEYS

    @pl.when(et == 0)
    def _():
        acc_scr[...] = jnp.zeros_like(acc_scr)

    hb = ht_ref[...]
    n_pair = PEER_ET // PEER_MM
    per_mm = PEER_MM // PEER_N_KEYS
    for pr in range(n_pair):
        ps = slice(pr * PEER_MM, (pr + 1) * PEER_MM)
        a_scr[ps, :] = jnp.dot(u_ref[ps, :], hb, preferred_element_type=F32)
    for pr in range(n_pair):
        ps = slice(pr * PEER_MM, (pr + 1) * PEER_MM)
        for half in range(per_mm):
            rs = slice(pr * PEER_MM + half * PEER_N_KEYS, pr * PEER_MM + (half + 1) * PEER_N_KEYS)
            i = et * n_sub + pr * per_mm + half
            a = a_scr[rs, :]
            act = a / (1.0 + jnp.exp2(a * (GELU_K1 + GELU_K3 * (a * a))))
            g = None
            for h in range(PEER_HEADS):
                c_row = c_ref[h, pl.ds(i, 1), :].astype(r2_ref.dtype)
                cf_row = cf_ref[h, pl.ds(i, 1), :].astype(e2_ref.dtype)
                t = jnp.where(r2_ref[h] < c_row, e2_ref[h] * cf_row, jnp.zeros((), e2_ref.dtype))
                g = t if g is None else g + t
            ga_scr[rs, :] = g.astype(BF16) * act.astype(BF16)
    acc_scr[...] += jnp.dot(vt_ref[...], ga_scr[...], preferred_element_type=F32)

    @pl.when(et == pl.num_programs(1) - 1)
    def _():
        o_ref[...] = x_ref[...] + acc_scr[...].T


def _peer_dense(x, ht, r2, e2, cnt, coef, u_tab, vt_tab, tb=512):
    T = x.shape[0]
    tb = min(tb, T)
    n_exp = u_tab.shape[0]
    aux_spec = pl.BlockSpec((PEER_HEADS, PEER_N_KEYS, tb), lambda i, e: (0, 0, i))
    return pl.pallas_call(
        _peer_dense_kernel,
        grid=(T // tb, n_exp // PEER_ET),
        in_specs=[pl.BlockSpec((tb, D_MODEL), lambda i, e: (i, 0)),
                  pl.BlockSpec((D_MODEL, tb), lambda i, e: (0, i)),
                  aux_spec, aux_spec, aux_spec, aux_spec,
                  pl.BlockSpec((PEER_ET, D_MODEL), lambda i, e: (e, 0)),
                  pl.BlockSpec((D_MODEL, PEER_ET), lambda i, e: (0, e))],
        out_specs=pl.BlockSpec((tb, D_MODEL), lambda i, e: (i, 0)),
        out_shape=jax.ShapeDtypeStruct((T, D_MODEL), F32),
        scratch_shapes=[pltpu.VMEM((D_MODEL, tb), F32),
                        pltpu.VMEM((PEER_ET, tb), F32),
                        pltpu.VMEM((PEER_ET, tb), BF16)],
        compiler_params=_cparams(("parallel", "arbitrary"), VMEM_LIMIT),
        name="peer_dense",
    )(x, ht, r2, e2, cnt, coef, u_tab, vt_tab)


def _final_kernel(x_ref, g_ref, o_ref):
    o_ref[...] = _rms(x_ref[...], g_ref[...])


def _final_norm(x, gain, tb=1024):
    T = x.shape[0]
    tb = min(tb, T)
    return pl.pallas_call(
        _final_kernel,
        grid=(T // tb,),
        in_specs=[pl.BlockSpec((tb, D_MODEL), lambda i: (i, 0)),
                  pl.BlockSpec((1, D_MODEL), lambda i: (0, 0))],
        out_specs=pl.BlockSpec((tb, D_MODEL), lambda i: (i, 0)),
        out_shape=jax.ShapeDtypeStruct((T, D_MODEL), F32),
        compiler_params=_cparams(("parallel",)),
        name="final_norm",
    )(x, gain)


def _rope_tables(positions):
    pos = positions.astype(F32).reshape(-1, 1)

    def tables(dim):
        half = dim // 2
        inv = ROPE_THETA ** (-jnp.arange(half, dtype=F32) / half)
        ang = pos * inv
        cos = jnp.cos(ang)
        sin = jnp.sin(ang)
        reps = LANES // dim
        return (jnp.tile(jnp.concatenate([cos, cos], axis=-1), (1, reps)),
                jnp.tile(jnp.concatenate([-sin, sin], axis=-1), (1, reps)))

    cos_c, sin_c = tables(C_HEAD_DIM)
    cos_i, sin_i = tables(IDX_DIM)
    return cos_c, sin_c, cos_i, sin_i


def _pack_w_in(w):
    n_idx = IDX_DIM + IDX_HEADS
    main = w[:, :N_MAIN]
    idx = w[:, N_MAIN:N_MAIN + n_idx]
    gates = w[:, N_MAIN + n_idx:]
    pad = jnp.zeros((w.shape[0], LANES - n_idx), w.dtype)
    return jnp.concatenate([gates, main, idx, pad], axis=1).astype(BF16)


def _layer(x, l, B, S, tabs, tri, lbp_all, p):
    proj = _inproj(x, p["norm1_gain"][l][None, :], _pack_w_in(p["w_in"][l]))
    y_a = _hgrn(proj, lbp_all[l], p["hgrn_norm_gain"][l][None, :], B, S)
    bias_full = jnp.repeat(p["gmlp_b_s"][l].T, B_BLOCK, axis=1)
    y_b = _gmlp(proj, p["gmlp_norm_gain"][l][None, :], p["gmlp_w_s"][l], bias_full)
    q_r, k_r, v_t, qi_r, ki_r = _dsa_prep(proj, *tabs)
    y_c = _dsa(q_r, qi_r, proj, k_r, v_t, ki_r, tri, B, S)
    x = _merge(y_a, y_b, y_c, proj, p["w_branch"][l].astype(BF16), p["w_out"][l].astype(BF16), x)
    sk = p["peer_sub_keys"][l].reshape(2 * PEER_HEADS, PEER_N_KEYS, LANES).astype(BF16)
    ht, r2, e2, cnt, coef = _peer_prep(x, p["norm2_gain"][l][None, :],
                                       p["peer_w_q"][l].T.astype(BF16), sk)
    return _peer_dense(x, ht, r2, e2, cnt, coef, p["peer_u"][l].astype(BF16),
                       p["peer_v"][l].T.astype(BF16))


def kernel(x, positions, norm1_gain, w_in, hgrn_lb_logits, hgrn_norm_gain, gmlp_norm_gain,
           gmlp_w_s, gmlp_b_s, w_branch, w_out, norm2_gain, peer_w_q, peer_sub_keys,
           peer_u, peer_v, final_gain):
    B, S, _ = x.shape
    depth = w_in.shape[0]
    p = dict(norm1_gain=norm1_gain, w_in=w_in, hgrn_norm_gain=hgrn_norm_gain,
             gmlp_norm_gain=gmlp_norm_gain, gmlp_w_s=gmlp_w_s, gmlp_b_s=gmlp_b_s,
             w_branch=w_branch, w_out=w_out, norm2_gain=norm2_gain, peer_w_q=peer_w_q,
             peer_sub_keys=peer_sub_keys, peer_u=peer_u, peer_v=peer_v)
    sm = jax.nn.softmax(hgrn_lb_logits.astype(F32), axis=0)
    cs = jnp.cumsum(sm, axis=0)
    lb = cs - cs[0:1]
    lbp_all = jnp.stack([jnp.log(lb), jnp.log1p(-lb), 1.0 - lb], axis=1)
    tabs = _rope_tables(positions)
    r = lax.broadcasted_iota(jnp.int32, (DSA_TK, DSA_TK), 0)
    c = lax.broadcasted_iota(jnp.int32, (DSA_TK, DSA_TK), 1)
    tri = (r >= c).astype(BF16)
    xt = x.reshape(B * S, D_MODEL)
    for l in range(depth):
        xt = _layer(xt, l, B, S, tabs, tri, lbp_all, p)
    return _final_norm(xt, final_gain[None, :]).reshape(B, S, D_MODEL)
```

```python
import functools

import numpy as np
import jax
import jax.numpy as jnp
from jax import lax
from jax.experimental import pallas as pl
from jax.experimental.pallas import tpu as pltpu

F32 = jnp.float32
BF16 = jnp.bfloat16

D_MODEL = 1024
CHUNK = 64
EPS = 1e-6
ROPE_THETA = 10000.0
A_HEADS = 4
A_DK = 128
B_GROUPS = 4
B_BLOCK = 128
C_HEADS = 4
C_HEAD_DIM = 128
IDX_HEADS = 8
IDX_DIM = 64
DSA_TOPK_MAX = 256
Q_BLOCK = 128
N_BRANCH = 3
BRANCH_WIDTH = 512
PEER_HEADS = 8
PEER_N_KEYS = 128
PEER_TOPK = 16

LANES = 128
VMEM_LIMIT = 56 * 1024 * 1024

N_MAIN = 10 * 512
N_PACKED = N_BRANCH * D_MODEL + N_MAIN + LANES
COLBLK_MAIN = (N_BRANCH * D_MODEL) // 512
COLBLK_IDX = (N_BRANCH * D_MODEL + N_MAIN) // LANES

NEG_BIG = -1e30
INT_MIN = -2147483648


def _cparams(sem, vmem=None):
    return pltpu.CompilerParams(dimension_semantics=sem, vmem_limit_bytes=vmem)


def _rms(x, gain):
    return x * lax.rsqrt(jnp.mean(x * x, axis=-1, keepdims=True) + EPS) * gain


def _dot_t(a, b):
    return lax.dot_general(a, b, (((1,), (1,)), ((), ())), preferred_element_type=F32)


def _inproj_kernel(x_ref, g_ref, w_ref, o_ref, h_scr):
    @pl.when(pl.program_id(1) == 0)
    def _():
        h_scr[...] = _rms(x_ref[...], g_ref[...]).astype(BF16)

    o_ref[...] = jnp.dot(h_scr[...], w_ref[...], preferred_element_type=F32)


def _inproj(x, gain, w_packed, tm=1024, tn=1664):
    T = x.shape[0]
    tm = min(tm, T)
    return pl.pallas_call(
        _inproj_kernel,
        grid=(T // tm, N_PACKED // tn),
        in_specs=[pl.BlockSpec((tm, D_MODEL), lambda i, j: (i, 0)),
                  pl.BlockSpec((1, D_MODEL), lambda i, j: (0, 0)),
                  pl.BlockSpec((D_MODEL, tn), lambda i, j: (0, j))],
        out_specs=pl.BlockSpec((tm, tn), lambda i, j: (i, j)),
        out_shape=jax.ShapeDtypeStruct((T, N_PACKED), F32),
        scratch_shapes=[pltpu.VMEM((tm, D_MODEL), BF16)],
        compiler_params=_cparams(("parallel", "arbitrary"), VMEM_LIMIT),
        name="inproj",
    )(x, gain, w_packed)


HG_C = 128
HG_SUB = 16


def _hgrn_kernel(q_ref, f_ref, i_ref, g_ref, lbp_ref, gain_ref, o_ref, st_scr):
    @pl.when(pl.program_id(1) == 0)
    def _():
        st_scr[...] = jnp.zeros_like(st_scr)

    C = HG_C
    row = lax.broadcasted_iota(jnp.int32, (C, 1), 0)
    rowm = lax.broadcasted_iota(jnp.int32, (C, C), 0)
    colm = lax.broadcasted_iota(jnp.int32, (C, C), 1)
    gain = gain_ref[...]

    for h in range(A_HEADS):
        sl = slice(h * A_DK, (h + 1) * A_DK)
        z = f_ref[:, sl]
        loglb = lbp_ref[0:1, sl]
        log1mlb = lbp_ref[1:2, sl]
        omlb = lbp_ref[2:3, sl]
        lsig = -(jnp.maximum(-z, 0.0) + jnp.log1p(jnp.exp(-jnp.abs(z))))
        cterm = log1mlb + lsig
        logf = jnp.maximum(loglb, cterm) + jnp.log1p(jnp.exp(-jnp.abs(loglb - cterm)))
        key = omlb * jax.nn.sigmoid(-z)
        qf = jax.nn.silu(q_ref[:, sl]) * (A_DK ** -0.5)
        vf = i_ref[:, sl]

        b = logf
        sh = 1
        while sh < C:
            b = b + jnp.where(row >= sh, pltpu.roll(b, sh, 0), 0.0)
            sh *= 2

        attn = jnp.zeros((C, C), F32)
        m = C // 2
        while m >= HG_SUB:
            ref = jnp.concatenate(
                [jnp.broadcast_to(b[j * 2 * m + m - 1:j * 2 * m + m, :], (2 * m, A_DK))
                 for j in range(C // (2 * m))], axis=0)
            qt = qf * jnp.exp(jnp.minimum(b - ref, 0.0))
            kt = key * jnp.exp(jnp.minimum(ref - b, 0.0))
            a = _dot_t(qt.astype(BF16), kt.astype(BF16))
            msk = ((rowm // (2 * m)) == (colm // (2 * m))) & ((rowm % (2 * m)) >= m) & ((colm % (2 * m)) < m)
            attn = attn + jnp.where(msk, a, 0.0)
            m //= 2

        nsub = C // HG_SUB
        for j in range(HG_SUB):
            kj = jnp.concatenate(
                [jnp.broadcast_to(key[i * HG_SUB + j:i * HG_SUB + j + 1, :], (HG_SUB, A_DK))
                 for i in range(nsub)], axis=0)
            bj = jnp.concatenate(
                [jnp.broadcast_to(b[i * HG_SUB + j:i * HG_SUB + j + 1, :], (HG_SUB, A_DK))
                 for i in range(nsub)], axis=0)
            d = jnp.exp(jnp.minimum(b - bj, 0.0)) * qf * kj
            red = jnp.sum(d, axis=-1, keepdims=True)
            hit = (colm == (rowm // HG_SUB) * HG_SUB + j) & ((rowm % HG_SUB) >= j)
            attn = jnp.where(hit, red, attn)

        st = st_scr[h]
        o = jnp.dot(attn.astype(BF16), vf.astype(BF16), preferred_element_type=F32)
        o = o + _dot_t((qf * jnp.exp(b)).astype(BF16), st.astype(BF16))
        blast = b[C - 1:C, :]
        kdec = key * jnp.exp(blast - b)
        upd = lax.dot_general(vf.astype(BF16), kdec.astype(BF16), (((0,), (0,)), ((), ())),
                              preferred_element_type=F32)
        st_scr[h] = st * jnp.exp(blast) + upd

        o = _rms(o, gain)
        o_ref[:, sl] = o * jax.nn.silu(g_ref[:, sl])


def _hgrn(proj, lbp, gain, B, S):
    nc = S // HG_C
    cb = COLBLK_MAIN

    def spec(k):
        return pl.BlockSpec((HG_C, 512), lambda b, c, k=k: (b * nc + c, cb + k))

    return pl.pallas_call(
        _hgrn_kernel,
        grid=(B, nc),
        in_specs=[spec(0), spec(1), spec(2), spec(3),
                  pl.BlockSpec((3, 512), lambda b, c: (0, 0)),
                  pl.BlockSpec((1, A_DK), lambda b, c: (0, 0))],
        out_specs=pl.BlockSpec((HG_C, 512), lambda b, c: (b * nc + c, 0)),
        out_shape=jax.ShapeDtypeStruct((B * S, 512), F32),
        scratch_shapes=[pltpu.VMEM((A_HEADS, A_DK, A_DK), F32)],
        compiler_params=_cparams(("parallel", "arbitrary")),
        name="hgrn",
    )(proj, proj, proj, proj, lbp, gain)


def _gmlp_kernel(u_ref, v_ref, g_ref, w_ref, b_ref, o_ref, *, nsub):
    r = lax.broadcasted_iota(jnp.int32, (B_BLOCK, B_BLOCK), 0)
    c = lax.broadcasted_iota(jnp.int32, (B_BLOCK, B_BLOCK), 1)
    tril = r >= c
    u = jax.nn.gelu(u_ref[...])
    v = _rms(jax.nn.gelu(v_ref[...]), g_ref[...]).astype(BF16)
    bias = b_ref[...]
    for g in range(B_GROUPS):
        w = jnp.where(tril, w_ref[g], 0.0).astype(BF16)
        cs = slice(g * LANES, (g + 1) * LANES)
        for s in range(nsub):
            rs = slice(s * B_BLOCK, (s + 1) * B_BLOCK)
            mixed = jnp.dot(w, v[rs, cs], preferred_element_type=F32) + bias[:, cs]
            o_ref[rs, cs] = u[rs, cs] * mixed


def _gmlp(proj, gain, w_s, bias_full, tb=512):
    T = proj.shape[0]
    tb = min(tb, T)
    cb = COLBLK_MAIN
    return pl.pallas_call(
        functools.partial(_gmlp_kernel, nsub=tb // B_BLOCK),
        grid=(T // tb,),
        in_specs=[pl.BlockSpec((tb, 512), lambda i: (i, cb + 4)),
                  pl.BlockSpec((tb, 512), lambda i: (i, cb + 5)),
                  pl.BlockSpec((1, 512), lambda i: (0, 0)),
                  pl.BlockSpec((B_GROUPS, B_BLOCK, B_BLOCK), lambda i: (0, 0, 0)),
                  pl.BlockSpec((B_BLOCK, 512), lambda i: (0, 0))],
        out_specs=pl.BlockSpec((tb, 512), lambda i: (i, 0)),
        out_shape=jax.ShapeDtypeStruct((T, 512), F32),
        compiler_params=_cparams(("parallel",)),
        name="gmlp",
    )(proj, proj, gain, w_s, bias_full)


def _dsa_prep_kernel(q_ref, k_ref, v_ref, iq_ref, idx_ref, cc_ref, sc_ref, ci_ref, si_ref,
                     qo_ref, ko_ref, vo_ref, qio_ref, kio_ref):
    cc = cc_ref[...]
    sc = sc_ref[...]
    ci = ci_ref[...]
    si = si_ref[...]
    lane = lax.broadcasted_iota(jnp.int32, ci.shape, 1)
    first_half = (lane % IDX_DIM) < (IDX_DIM // 2)

    def rope_i(x):
        partner = jnp.where(first_half, pltpu.roll(x, LANES - IDX_DIM // 2, 1),
                            pltpu.roll(x, IDX_DIM // 2, 1))
        return x * ci + partner * si

    for h in range(C_HEADS):
        sl = slice(h * LANES, (h + 1) * LANES)
        x = q_ref[:, sl]
        qo_ref[:, sl] = ((x * cc + pltpu.roll(x, C_HEAD_DIM // 2, 1) * sc)
                         * (C_HEAD_DIM ** -0.5)).astype(BF16)
        x = k_ref[:, sl]
        ko_ref[:, sl] = (x * cc + pltpu.roll(x, C_HEAD_DIM // 2, 1) * sc).astype(BF16)
    vo_ref[0] = v_ref[...].T.astype(BF16)
    for j in range(IDX_HEADS // 2):
        r = rope_i(iq_ref[:, j * LANES:(j + 1) * LANES])
        qio_ref[:, (2 * j) * LANES:(2 * j + 1) * LANES] = r.astype(BF16)
        qio_ref[:, (2 * j + 1) * LANES:(2 * j + 2) * LANES] = pltpu.roll(r, IDX_DIM, 1).astype(BF16)
    rk = rope_i(idx_ref[...])
    kio_ref[...] = jnp.where(lane < IDX_DIM, rk, 0.0).astype(BF16)


def _dsa_prep(proj, cos_c, sin_c, cos_i, sin_i):
    T = proj.shape[0]
    tb = DSA_TK
    cb = COLBLK_MAIN

    def pspec(k):
        return pl.BlockSpec((tb, 512), lambda i, k=k: (i, cb + k))

    tspec = pl.BlockSpec((tb, LANES), lambda i: (i, 0))
    return pl.pallas_call(
        _dsa_prep_kernel,
        grid=(T // tb,),
        in_specs=[pspec(6), pspec(7), pspec(8), pspec(9),
                  pl.BlockSpec((tb, LANES), lambda i: (i, COLBLK_IDX)),
                  tspec, tspec, tspec, tspec],
        out_specs=[pl.BlockSpec((tb, 512), lambda i: (i, 0)),
                   pl.BlockSpec((tb, 512), lambda i: (i, 0)),
                   pl.BlockSpec((1, 512, tb), lambda i: (i, 0, 0)),
                   pl.BlockSpec((tb, IDX_HEADS * LANES), lambda i: (i, 0)),
                   pl.BlockSpec((tb, LANES), lambda i: (i, 0))],
        out_shape=[jax.ShapeDtypeStruct((T, 512), BF16),
                   jax.ShapeDtypeStruct((T, 512), BF16),
                   jax.ShapeDtypeStruct((T // tb, 512, tb), BF16),
                   jax.ShapeDtypeStruct((T, IDX_HEADS * LANES), BF16),
                   jax.ShapeDtypeStruct((T, LANES), BF16)],
        compiler_params=_cparams(("parallel",)),
        name="dsa_prep",
    )(proj, proj, proj, proj, proj, cos_c, sin_c, cos_i, sin_i)


DSA_TK = 512


def _sortable(score):
    score = jnp.where(score == 0.0, 0.0, score)
    bits = pltpu.bitcast(score, jnp.int32)
    return bits ^ ((bits >> 31) & 0x7FFFFFFF)


DSA_NACC = 128
I16_MIN = -32768


def _dsa_kernel(q_ref, qi_ref, idx_ref, k_ref, vt_ref, ki_ref, trit_ref, o_ref,
                qidx_scr, key_scr, hi_scr, lo_scr, acc_scr, *, topk):
    TK = DSA_TK
    I16 = jnp.int16
    qb = pl.program_id(1)
    nkb = (qb * Q_BLOCK + Q_BLOCK + TK - 1) // TK
    wscale = (IDX_HEADS ** -0.5) * (IDX_DIM ** -0.5)

    for h in range(IDX_HEADS):
        qidx_scr[h * Q_BLOCK:(h + 1) * Q_BLOCK, :] = qi_ref[:, h * LANES:(h + 1) * LANES]
    iwt = idx_ref[...].T
    w_rows = [iwt[IDX_DIM + h:IDX_DIM + h + 1, :] * wscale for h in range(IDX_HEADS)]
    qchunk = (qb * Q_BLOCK + lax.broadcasted_iota(jnp.int32, (1, Q_BLOCK), 1)) // CHUNK
    row_tk = lax.broadcasted_iota(jnp.int32, (TK, 1), 0)

    def p1(kb, carry):
        off = pl.multiple_of(kb * TK, TK)
        logits = _dot_t(ki_ref[pl.ds(off, TK), :], qidx_scr[...])
        s = None
        for h in range(IDX_HEADS):
            t = jnp.maximum(logits[:, h * Q_BLOCK:(h + 1) * Q_BLOCK], 0.0) * w_rows[h]
            s = t if s is None else s + t
        adm = ((off + row_tk) // CHUNK) <= qchunk
        key = _sortable(jnp.where(adm, s, -jnp.inf))
        key_scr[kb] = key
        hi_scr[kb] = (key >> 16).astype(I16)
        lo_scr[kb] = ((key & 0xFFFF) + I16_MIN).astype(I16)
        return carry

    lax.fori_loop(0, nkb, p1, 0)

    def count16(ref, cand, strict):
        c16 = cand.astype(I16)

        def body(kb, acc):
            t = ref[kb]
            for r in range(TK // DSA_NACC):
                blk = t[r * DSA_NACC:(r + 1) * DSA_NACC]
                hit = (blk > c16) if strict else (blk >= c16)
                acc = acc + jnp.where(hit, jnp.ones((), I16), jnp.zeros((), I16))
            return acc

        acc = lax.fori_loop(0, nkb, body, jnp.zeros((DSA_NACC, LANES), I16))
        return jnp.sum(acc.astype(jnp.int32), axis=0, keepdims=True)

    def bisect16(ref, kneed):
        zero = jnp.zeros((1, Q_BLOCK), jnp.int32)
        base = jnp.where(count16(ref, zero, False) >= kneed, zero, zero + I16_MIN)

        def step(i, base):
            cand = base | (jnp.int32(1) << (14 - i))
            return jnp.where(count16(ref, cand, False) >= kneed, cand, base)

        return lax.fori_loop(0, 15, step, base)

    tau_hi = bisect16(hi_scr, topk)
    cnt_gt_hi = count16(hi_scr, tau_hi, True)
    th16 = tau_hi.astype(I16)

    def keep_group(kb, carry):
        lo_scr[kb] = jnp.where(hi_scr[kb] == th16, lo_scr[kb], jnp.full((), I16_MIN, I16))
        return carry

    lax.fori_loop(0, nkb, keep_group, 0)
    tau_lo = bisect16(lo_scr, topk - cnt_gt_hi)
    cnt_gt = cnt_gt_hi + count16(lo_scr, tau_lo, True)
    tau = (tau_hi << 16) | (tau_lo - I16_MIN)
    need = (topk - cnt_gt).astype(F32)
    key_neg_inf = _sortable(jnp.full((1, 1), -jnp.inf, F32))
    tau_eff = jnp.maximum(tau, key_neg_inf + 1)

    acc_scr[...] = jnp.zeros_like(acc_scr)
    hsl = [slice(h * LANES, (h + 1) * LANES) for h in range(C_HEADS)]

    def p2(kb, carry, tie_order):
        eq_seen, ms, ls = carry
        off = pl.multiple_of(kb * TK, TK)
        kt = key_scr[kb]
        if tie_order:
            eq = kt == tau
            prefix = jnp.dot(trit_ref[...], jnp.where(eq, 1.0, 0.0).astype(BF16),
                             preferred_element_type=F32)
        ss = [_dot_t(k_ref[pl.ds(off, TK), sl], q_ref[:, sl]) for sl in hsl]
        if tie_order:
            over = jnp.where(eq_seen + prefix > need, 1, 0)
            kt = jnp.where(eq, kt - over, kt)
            eq_seen = eq_seen + prefix[TK - 1:TK, :]
        bias = jnp.where(kt >= tau_eff, 0.0, NEG_BIG)
        new_ms, new_ls, alphas, ps = [], [], [], []
        for h in range(C_HEADS):
            s = ss[h] + bias
            m_new = jnp.maximum(ms[h], jnp.max(s, axis=0, keepdims=True))
            alpha = jnp.exp(ms[h] - m_new)
            p = jnp.exp(s - m_new)
            new_ls.append(alpha * ls[h] + jnp.sum(p, axis=0, keepdims=True))
            new_ms.append(m_new)
            alphas.append(alpha)
            ps.append(p.astype(BF16))
        for h in range(C_HEADS):
            acc_scr[h] = alphas[h] * acc_scr[h] + jnp.dot(vt_ref[kb, hsl[h], :], ps[h],
                                                          preferred_element_type=F32)
        return eq_seen, tuple(new_ms), tuple(new_ls)

    row0 = jnp.zeros((1, Q_BLOCK), F32)
    init = (row0,
            tuple(row0 + NEG_BIG for _ in range(C_HEADS)),
            tuple(row0 for _ in range(C_HEADS)))
    cnt_ge = cnt_gt_hi + count16(lo_scr, tau_lo, False)
    _, ms, ls = lax.cond(
        jnp.max(cnt_ge) > topk,
        lambda: lax.fori_loop(0, nkb, functools.partial(p2, tie_order=True), init),
        lambda: lax.fori_loop(0, nkb, functools.partial(p2, tie_order=False), init))
    for h in range(C_HEADS):
        o_ref[:, h * LANES:(h + 1) * LANES] = (acc_scr[h] / ls[h]).T


def _dsa(q_r, qi_r, proj, k_r, v_t, ki_r, trit, B, S):
    nq = S // Q_BLOCK
    topk = min(DSA_TOPK_MAX, S // 4)
    nkt = S // DSA_TK
    assert nkt * DSA_TK == S, "sequence length must be a multiple of the key tile"
    return pl.pallas_call(
        functools.partial(_dsa_kernel, topk=topk),
        grid=(B, nq),
        in_specs=[pl.BlockSpec((Q_BLOCK, 512), lambda b, q: (b * nq + q, 0)),
                  pl.BlockSpec((Q_BLOCK, IDX_HEADS * LANES), lambda b, q: (b * nq + q, 0)),
                  pl.BlockSpec((Q_BLOCK, LANES), lambda b, q: (b * nq + q, COLBLK_IDX)),
                  pl.BlockSpec((S, 512), lambda b, q: (b, 0)),
                  pl.BlockSpec((nkt, 512, DSA_TK), lambda b, q: (b, 0, 0)),
                  pl.BlockSpec((S, LANES), lambda b, q: (b, 0)),
                  pl.BlockSpec((DSA_TK, DSA_TK), lambda b, q: (0, 0))],
        out_specs=pl.BlockSpec((Q_BLOCK, 512), lambda b, q: (b * nq + q, 0)),
        out_shape=jax.ShapeDtypeStruct((B * S, 512), F32),
        scratch_shapes=[pltpu.VMEM((IDX_HEADS * Q_BLOCK, LANES), BF16),
                        pltpu.VMEM((nkt, DSA_TK, Q_BLOCK), jnp.int32),
                        pltpu.VMEM((nkt, DSA_TK, Q_BLOCK), jnp.int16),
                        pltpu.VMEM((nkt, DSA_TK, Q_BLOCK), jnp.int16),
                        pltpu.VMEM((C_HEADS, LANES, Q_BLOCK), F32)],
        compiler_params=_cparams(("parallel", "arbitrary"), VMEM_LIMIT),
        name="dsa",
    )(q_r, qi_r, proj, k_r, v_t, ki_r, trit)


def _merge_kernel(ya_ref, yb_ref, yc_ref, g0_ref, g1_ref, g2_ref, wb_ref, wo_ref, x_ref, o_ref):
    mixed = None
    for y_ref, g_ref, n in ((ya_ref, g0_ref, 0), (yb_ref, g1_ref, 1), (yc_ref, g2_ref, 2)):
        bp = jnp.dot(y_ref[...].astype(BF16), wb_ref[n], preferred_element_type=F32)
        t = jax.nn.sigmoid(g_ref[...]) * bp
        mixed = t if mixed is None else mixed + t
    o_ref[...] = x_ref[...] + jnp.dot(mixed.astype(BF16), wo_ref[...], preferred_element_type=F32)


def _merge(y_a, y_b, y_c, proj, w_branch, w_out, x, tb=256):
    T = x.shape[0]
    tb = min(tb, T)
    yspec = pl.BlockSpec((tb, 512), lambda i: (i, 0))

    def gspec(n):
        return pl.BlockSpec((tb, D_MODEL), lambda i, n=n: (i, n))

    return pl.pallas_call(
        _merge_kernel,
        grid=(T // tb,),
        in_specs=[yspec, yspec, yspec, gspec(0), gspec(1), gspec(2),
                  pl.BlockSpec((N_BRANCH, BRANCH_WIDTH, D_MODEL), lambda i: (0, 0, 0)),
                  pl.BlockSpec((D_MODEL, D_MODEL), lambda i: (0, 0)),
                  pl.BlockSpec((tb, D_MODEL), lambda i: (i, 0))],
        out_specs=pl.BlockSpec((tb, D_MODEL), lambda i: (i, 0)),
        out_shape=jax.ShapeDtypeStruct((T, D_MODEL), F32),
        compiler_params=_cparams(("parallel",), VMEM_LIMIT),
        name="merge",
    )(y_a, y_b, y_c, proj, proj, proj, w_branch, w_out, x)


def _oddeven_merge(lo, hi, r):
    step = r * 2
    if step < hi - lo:
        yield from _oddeven_merge(lo, hi, step)
        yield from _oddeven_merge(lo + r, hi, step)
        yield from [(i, i + r) for i in range(lo + r, hi - r, step)]
    else:
        yield (lo, lo + r)


def _oddeven_sort(lo, hi):
    if hi - lo >= 1:
        mid = lo + (hi - lo) // 2
        yield from _oddeven_sort(lo, mid)
        yield from _oddeven_sort(mid + 1, hi)
        yield from _oddeven_merge(lo, hi, 1)


SUBLANES = 8
NET16 = tuple(_oddeven_sort(0, PEER_TOPK - 1))
NET8 = tuple(_oddeven_sort(0, SUBLANES - 1))
PEER_CAND = tuple((k1, k2) for k1 in range(PEER_TOPK) for k2 in range(PEER_TOPK)
                  if (k1 + 1) * (k2 + 1) <= PEER_TOPK)


def _ce(x, i, j):
    a, b = x[i], x[j]
    x[i] = jnp.maximum(a, b)
    x[j] = jnp.minimum(a, b)


def _bitonic_sort16(m):
    d = PEER_TOPK // 2
    while d >= 1:
        for i in range(PEER_TOPK):
            if (i & d) == 0:
                _ce(m, i, i + d)
        d //= 2


def _merge_sublanes(v, shifts):
    for sh in shifts:
        w = [pltpu.roll(t, sh, 0) for t in v]
        v = [jnp.maximum(v[k], w[PEER_TOPK - 1 - k]) for k in range(PEER_TOPK)]
        _bitonic_sort16(v)
    return v


def _top16_of_rows(s):
    v = [s[SUBLANES * k:SUBLANES * (k + 1), :] for k in range(PEER_N_KEYS // SUBLANES)]
    for i, j in NET16:
        _ce(v, i, j)
    return _merge_sublanes(v, (4, 2, 1))


def _peer_prep_kernel(x_ref, g_ref, wqt_ref, sk_ref, ht_ref, r2_ref, e2_ref, c_ref, cf_ref, hb_scr):
    tb = x_ref.shape[0]
    h2 = _rms(x_ref[...], g_ref[...])
    hb_scr[...] = h2.T.astype(BF16)
    ht_ref[...] = hb_scr[...]
    sub = lax.broadcasted_iota(jnp.int32, (SUBLANES, LANES), 0)
    neg = jnp.full((SUBLANES, LANES), -jnp.inf, F32)

    def head(h, carry):
        hb = hb_scr[...]
        q1 = jnp.dot(wqt_ref[pl.ds(pl.multiple_of(h * 2 * LANES, LANES), LANES), :], hb,
                     preferred_element_type=F32)
        q2 = jnp.dot(wqt_ref[pl.ds(pl.multiple_of(h * 2 * LANES + LANES, LANES), LANES), :], hb,
                     preferred_element_type=F32)
        s1f = jnp.dot(sk_ref[2 * h], q1.astype(BF16), preferred_element_type=F32)
        s2f = jnp.dot(sk_ref[2 * h + 1], q2.astype(BF16), preferred_element_type=F32)
        for lt in range(tb // LANES):
            ls = slice(lt * LANES, (lt + 1) * LANES)
            s1 = s1f[:, ls]
            s2 = s2f[:, ls]
            a = _top16_of_rows(s1)
            b = _top16_of_rows(s2)
            packed = []
            for v in range((len(PEER_CAND) + SUBLANES - 1) // SUBLANES):
                p = neg
                for s in range(SUBLANES):
                    c = SUBLANES * v + s
                    if c < len(PEER_CAND):
                        k1, k2 = PEER_CAND[c]
                        p = jnp.where(sub == s, a[k1] + b[k2], p)
                packed.append(p)
            while len(packed) < SUBLANES:
                packed.append(neg)
            for i, j in NET8:
                _ce(packed, i, j)
            w = [pltpu.roll(t, 4, 0) for t in packed]
            m = packed + w[::-1]
            _bitonic_sort16(m)
            m = _merge_sublanes(m, (2, 1))
            thr = m[PEER_TOPK - 1][0:1, :]
            zsum = m[0] - m[0] + 1.0
            for k in range(1, PEER_TOPK):
                zsum = zsum + jnp.exp(m[k] - m[0])
            zinv = 1.0 / zsum[0:1, :]
            cnt = jnp.zeros((PEER_N_KEYS, LANES), F32)
            rank = jnp.zeros((PEER_N_KEYS, LANES), F32)
            for k in range(PEER_TOPK):
                bk = b[k][0:1, :]
                cnt = cnt + jnp.where(s1 + bk >= thr, 1.0, 0.0)
                rank = rank + jnp.where(bk > s2, 1.0, 0.0)
            r2_ref[h, :, ls] = rank.astype(r2_ref.dtype)
            e2_ref[h, :, ls] = jnp.exp(s2 - b[0][0:1, :]).astype(e2_ref.dtype)
            c_ref[h, :, ls] = cnt
            cf_ref[h, :, ls] = jnp.exp(s1 - a[0][0:1, :]) * zinv
        return carry

    lax.fori_loop(0, PEER_HEADS, head, 0)


def _peer_prep(x, gain, w_qt, sub_keys, tb=256):
    T = x.shape[0]
    tb = min(tb, T)
    aux_spec = pl.BlockSpec((PEER_HEADS, PEER_N_KEYS, tb), lambda i: (0, 0, i))
    aux_f32 = jax.ShapeDtypeStruct((PEER_HEADS, PEER_N_KEYS, T), F32)
    aux_b16 = jax.ShapeDtypeStruct((PEER_HEADS, PEER_N_KEYS, T), BF16)
    return pl.pallas_call(
        _peer_prep_kernel,
        grid=(T // tb,),
        in_specs=[pl.BlockSpec((tb, D_MODEL), lambda i: (i, 0)),
                  pl.BlockSpec((1, D_MODEL), lambda i: (0, 0)),
                  pl.BlockSpec((2 * PEER_HEADS * LANES, D_MODEL), lambda i: (0, 0)),
                  pl.BlockSpec((2 * PEER_HEADS, PEER_N_KEYS, LANES), lambda i: (0, 0, 0))],
        out_specs=[pl.BlockSpec((D_MODEL, tb), lambda i: (0, i)),
                   aux_spec, aux_spec, aux_spec, aux_spec],
        out_shape=[jax.ShapeDtypeStruct((D_MODEL, T), BF16),
                   aux_b16, aux_b16, aux_f32, aux_f32],
        scratch_shapes=[pltpu.VMEM((D_MODEL, tb), BF16)],
        compiler_params=_cparams(("parallel",), VMEM_LIMIT),
        name="peer_prep",
    )(x, gain, w_qt, sub_keys)


PEER_ET = 2048
PEER_MM = 256


GELU_K1 = -2.0 * float(np.sqrt(2.0 / np.pi)) * float(np.log2(np.e))
GELU_K3 = GELU_K1 * 0.044715


def _peer_dense_kernel(x_ref, ht_ref, r2_ref, e2_ref, c_ref, cf_ref, u_ref, vt_ref, o_ref,
                       acc_scr, a_scr, ga_scr):
    et = pl.program_id(1)
    n_sub = PEER_ET // PEER_N_KEYS

    @pl.when(et == 0)
    def _():
        acc_scr[...] = jnp.zeros_like(acc_scr)

    hb = ht_ref[...]
    n_pair = PEER_ET // PEER_MM
    per_mm = PEER_MM // PEER_N_KEYS
    for pr in range(n_pair):
        ps = slice(pr * PEER_MM, (pr + 1) * PEER_MM)
        a_scr[ps, :] = jnp.dot(u_ref[ps, :], hb, preferred_element_type=F32)
    for pr in range(n_pair):
        ps = slice(pr * PEER_MM, (pr + 1) * PEER_MM)
        for half in range(per_mm):
            rs = slice(pr * PEER_MM + half * PEER_N_KEYS, pr * PEER_MM + (half + 1) * PEER_N_KEYS)
            i = et * n_sub + pr * per_mm + half
            a = a_scr[rs, :].astype(BF16)
            act = a / (1.0 + jnp.exp2(a * (GELU_K1 + GELU_K3 * (a * a))))
            g = None
            for h in range(PEER_HEADS):
                c_row = c_ref[h, pl.ds(i, 1), :].astype(r2_ref.dtype)
                cf_row = cf_ref[h, pl.ds(i, 1), :].astype(e2_ref.dtype)
                t = jnp.where(r2_ref[h] < c_row, e2_ref[h] * cf_row, jnp.zeros((), e2_ref.dtype))
                g = t if g is None else g + t
            ga_scr[rs, :] = g.astype(BF16) * act
    acc_scr[...] += jnp.dot(vt_ref[...], ga_scr[...], preferred_element_type=F32)

    @pl.when(et == pl.num_programs(1) - 1)
    def _():
        o_ref[...] = x_ref[...] + acc_scr[...].T


def _peer_dense(x, ht, r2, e2, cnt, coef, u_tab, vt_tab, tb=512):
    T = x.shape[0]
    tb = min(tb, T)
    n_exp = u_tab.shape[0]
    aux_spec = pl.BlockSpec((PEER_HEADS, PEER_N_KEYS, tb), lambda i, e: (0, 0, i))
    return pl.pallas_call(
        _peer_dense_kernel,
        grid=(T // tb, n_exp // PEER_ET),
        in_specs=[pl.BlockSpec((tb, D_MODEL), lambda i, e: (i, 0)),
                  pl.BlockSpec((D_MODEL, tb), lambda i, e: (0, i)),
                  aux_spec, aux_spec, aux_spec, aux_spec,
                  pl.BlockSpec((PEER_ET, D_MODEL), lambda i, e: (e, 0)),
                  pl.BlockSpec((D_MODEL, PEER_ET), lambda i, e: (0, e))],
        out_specs=pl.BlockSpec((tb, D_MODEL), lambda i, e: (i, 0)),
        out_shape=jax.ShapeDtypeStruct((T, D_MODEL), F32),
        scratch_shapes=[pltpu.VMEM((D_MODEL, tb), F32),
                        pltpu.VMEM((PEER_ET, tb), F32),
                        pltpu.VMEM((PEER_ET, tb), BF16)],
        compiler_params=_cparams(("parallel", "arbitrary"), VMEM_LIMIT),
        name="peer_dense",
    )(x, ht, r2, e2, cnt, coef, u_tab, vt_tab)


def _final_kernel(x_ref, g_ref, o_ref):
    o_ref[...] = _rms(x_ref[...], g_ref[...])


def _final_norm(x, gain, tb=1024):
    T = x.shape[0]
    tb = min(tb, T)
    return pl.pallas_call(
        _final_kernel,
        grid=(T // tb,),
        in_specs=[pl.BlockSpec((tb, D_MODEL), lambda i: (i, 0)),
                  pl.BlockSpec((1, D_MODEL), lambda i: (0, 0))],
        out_specs=pl.BlockSpec((tb, D_MODEL), lambda i: (i, 0)),
        out_shape=jax.ShapeDtypeStruct((T, D_MODEL), F32),
        compiler_params=_cparams(("parallel",)),
        name="final_norm",
    )(x, gain)


def _rope_tables(positions):
    pos = positions.astype(F32).reshape(-1, 1)

    def tables(dim):
        half = dim // 2
        inv = ROPE_THETA ** (-jnp.arange(half, dtype=F32) / half)
        ang = pos * inv
        cos = jnp.cos(ang)
        sin = jnp.sin(ang)
        reps = LANES // dim
        return (jnp.tile(jnp.concatenate([cos, cos], axis=-1), (1, reps)),
                jnp.tile(jnp.concatenate([-sin, sin], axis=-1), (1, reps)))

    cos_c, sin_c = tables(C_HEAD_DIM)
    cos_i, sin_i = tables(IDX_DIM)
    return cos_c, sin_c, cos_i, sin_i


def _pack_w_in(w):
    n_idx = IDX_DIM + IDX_HEADS
    main = w[:, :N_MAIN]
    idx = w[:, N_MAIN:N_MAIN + n_idx]
    gates = w[:, N_MAIN + n_idx:]
    pad = jnp.zeros((w.shape[0], LANES - n_idx), w.dtype)
    return jnp.concatenate([gates, main, idx, pad], axis=1).astype(BF16)


def _layer(x, l, B, S, tabs, tri, lbp_all, p):
    proj = _inproj(x, p["norm1_gain"][l][None, :], _pack_w_in(p["w_in"][l]))
    y_a = _hgrn(proj, lbp_all[l], p["hgrn_norm_gain"][l][None, :], B, S)
    bias_full = jnp.repeat(p["gmlp_b_s"][l].T, B_BLOCK, axis=1)
    y_b = _gmlp(proj, p["gmlp_norm_gain"][l][None, :], p["gmlp_w_s"][l], bias_full)
    q_r, k_r, v_t, qi_r, ki_r = _dsa_prep(proj, *tabs)
    y_c = _dsa(q_r, qi_r, proj, k_r, v_t, ki_r, tri, B, S)
    x = _merge(y_a, y_b, y_c, proj, p["w_branch"][l].astype(BF16), p["w_out"][l].astype(BF16), x)
    sk = p["peer_sub_keys"][l].reshape(2 * PEER_HEADS, PEER_N_KEYS, LANES).astype(BF16)
    ht, r2, e2, cnt, coef = _peer_prep(x, p["norm2_gain"][l][None, :],
                                       p["peer_w_q"][l].T.astype(BF16), sk)
    return _peer_dense(x, ht, r2, e2, cnt, coef, p["peer_u"][l].astype(BF16),
                       p["peer_v"][l].T.astype(BF16))


def kernel(x, positions, norm1_gain, w_in, hgrn_lb_logits, hgrn_norm_gain, gmlp_norm_gain,
           gmlp_w_s, gmlp_b_s, w_branch, w_out, norm2_gain, peer_w_q, peer_sub_keys,
           peer_u, peer_v, final_gain):
    B, S, _ = x.shape
    depth = w_in.shape[0]
    p = dict(norm1_gain=norm1_gain, w_in=w_in, hgrn_norm_gain=hgrn_norm_gain,
             gmlp_norm_gain=gmlp_norm_gain, gmlp_w_s=gmlp_w_s, gmlp_b_s=gmlp_b_s,
             w_branch=w_branch, w_out=w_out, norm2_gain=norm2_gain, peer_w_q=peer_w_q,
             peer_sub_keys=peer_sub_keys, peer_u=peer_u, peer_v=peer_v)
    sm = jax.nn.softmax(hgrn_lb_logits.astype(F32), axis=0)
    cs = jnp.cumsum(sm, axis=0)
    lb = cs - cs[0:1]
    lbp_all = jnp.stack([jnp.log(lb), jnp.log1p(-lb), 1.0 - lb], axis=1)
    tabs = _rope_tables(positions)
    r = lax.broadcasted_iota(jnp.int32, (DSA_TK, DSA_TK), 0)
    c = lax.broadcasted_iota(jnp.int32, (DSA_TK, DSA_TK), 1)
    tri = (r >= c).astype(BF16)
    xt = x.reshape(B * S, D_MODEL)
    for l in range(depth):
        xt = _layer(xt, l, B, S, tabs, tri, lbp_all, p)
    return _final_norm(xt, final_gain[None, :]).reshape(B, S, D_MODEL)
```

```python
import functools

import numpy as np
import jax
import jax.numpy as jnp
from jax import lax
from jax.experimental import pallas as pl
from jax.experimental.pallas import tpu as pltpu

F32 = jnp.float32
BF16 = jnp.bfloat16

D_MODEL = 1024
CHUNK = 64
EPS = 1e-6
ROPE_THETA = 10000.0
A_HEADS = 4
A_DK = 128
B_GROUPS = 4
B_BLOCK = 128
C_HEADS = 4
C_HEAD_DIM = 128
IDX_HEADS = 8
IDX_DIM = 64
DSA_TOPK_MAX = 256
Q_BLOCK = 128
N_BRANCH = 3
BRANCH_WIDTH = 512
PEER_HEADS = 8
PEER_N_KEYS = 128
PEER_TOPK = 16

LANES = 128
VMEM_LIMIT = 56 * 1024 * 1024

N_MAIN = 10 * 512
N_PACKED = N_BRANCH * D_MODEL + N_MAIN + LANES
COLBLK_MAIN = (N_BRANCH * D_MODEL) // 512
COLBLK_IDX = (N_BRANCH * D_MODEL + N_MAIN) // LANES

NEG_BIG = -1e30
INT_MIN = -2147483648


def _cparams(sem, vmem=None):
    return pltpu.CompilerParams(dimension_semantics=sem, vmem_limit_bytes=vmem)


def _rms(x, gain):
    return x * lax.rsqrt(jnp.mean(x * x, axis=-1, keepdims=True) + EPS) * gain


def _dot_t(a, b):
    return lax.dot_general(a, b, (((1,), (1,)), ((), ())), preferred_element_type=F32)


def _inproj_kernel(x_ref, g_ref, w_ref, o_ref, h_scr):
    @pl.when(pl.program_id(1) == 0)
    def _():
        h_scr[...] = _rms(x_ref[...], g_ref[...]).astype(BF16)

    o_ref[...] = jnp.dot(h_scr[...], w_ref[...], preferred_element_type=F32)


def _inproj(x, gain, w_packed, tm=1024, tn=1664):
    T = x.shape[0]
    tm = min(tm, T)
    return pl.pallas_call(
        _inproj_kernel,
        grid=(T // tm, N_PACKED // tn),
        in_specs=[pl.BlockSpec((tm, D_MODEL), lambda i, j: (i, 0)),
                  pl.BlockSpec((1, D_MODEL), lambda i, j: (0, 0)),
                  pl.BlockSpec((D_MODEL, tn), lambda i, j: (0, j))],
        out_specs=pl.BlockSpec((tm, tn), lambda i, j: (i, j)),
        out_shape=jax.ShapeDtypeStruct((T, N_PACKED), F32),
        scratch_shapes=[pltpu.VMEM((tm, D_MODEL), BF16)],
        compiler_params=_cparams(("parallel", "arbitrary"), VMEM_LIMIT),
        name="inproj",
    )(x, gain, w_packed)


HG_C = 128
HG_SUB = 16


def _hgrn_kernel(q_ref, f_ref, i_ref, g_ref, lbp_ref, gain_ref, o_ref, st_scr):
    @pl.when(pl.program_id(1) == 0)
    def _():
        st_scr[...] = jnp.zeros_like(st_scr)

    C = HG_C
    row = lax.broadcasted_iota(jnp.int32, (C, 1), 0)
    rowm = lax.broadcasted_iota(jnp.int32, (C, C), 0)
    colm = lax.broadcasted_iota(jnp.int32, (C, C), 1)
    gain = gain_ref[...]

    for h in range(A_HEADS):
        sl = slice(h * A_DK, (h + 1) * A_DK)
        z = f_ref[:, sl]
        loglb = lbp_ref[0:1, sl]
        log1mlb = lbp_ref[1:2, sl]
        omlb = lbp_ref[2:3, sl]
        lsig = -(jnp.maximum(-z, 0.0) + jnp.log1p(jnp.exp(-jnp.abs(z))))
        cterm = log1mlb + lsig
        logf = jnp.maximum(loglb, cterm) + jnp.log1p(jnp.exp(-jnp.abs(loglb - cterm)))
        key = omlb * jax.nn.sigmoid(-z)
        qf = jax.nn.silu(q_ref[:, sl]) * (A_DK ** -0.5)
        vf = i_ref[:, sl]

        b = logf
        sh = 1
        while sh < C:
            b = b + jnp.where(row >= sh, pltpu.roll(b, sh, 0), 0.0)
            sh *= 2

        attn = jnp.zeros((C, C), F32)
        m = C // 2
        while m >= HG_SUB:
            ref = jnp.concatenate(
                [jnp.broadcast_to(b[j * 2 * m + m - 1:j * 2 * m + m, :], (2 * m, A_DK))
                 for j in range(C // (2 * m))], axis=0)
            qt = qf * jnp.exp(jnp.minimum(b - ref, 0.0))
            kt = key * jnp.exp(jnp.minimum(ref - b, 0.0))
            a = _dot_t(qt.astype(BF16), kt.astype(BF16))
            msk = ((rowm // (2 * m)) == (colm // (2 * m))) & ((rowm % (2 * m)) >= m) & ((colm % (2 * m)) < m)
            attn = attn + jnp.where(msk, a, 0.0)
            m //= 2

        nsub = C // HG_SUB
        for j in range(HG_SUB):
            kj = jnp.concatenate(
                [jnp.broadcast_to(key[i * HG_SUB + j:i * HG_SUB + j + 1, :], (HG_SUB, A_DK))
                 for i in range(nsub)], axis=0)
            bj = jnp.concatenate(
                [jnp.broadcast_to(b[i * HG_SUB + j:i * HG_SUB + j + 1, :], (HG_SUB, A_DK))
                 for i in range(nsub)], axis=0)
            d = jnp.exp(jnp.minimum(b - bj, 0.0)) * qf * kj
            red = jnp.sum(d, axis=-1, keepdims=True)
            hit = (colm == (rowm // HG_SUB) * HG_SUB + j) & ((rowm % HG_SUB) >= j)
            attn = jnp.where(hit, red, attn)

        st = st_scr[h]
        o = jnp.dot(attn.astype(BF16), vf.astype(BF16), preferred_element_type=F32)
        o = o + _dot_t((qf * jnp.exp(b)).astype(BF16), st.astype(BF16))
        blast = b[C - 1:C, :]
        kdec = key * jnp.exp(blast - b)
        upd = lax.dot_general(vf.astype(BF16), kdec.astype(BF16), (((0,), (0,)), ((), ())),
                              preferred_element_type=F32)
        st_scr[h] = st * jnp.exp(blast) + upd

        o = _rms(o, gain)
        o_ref[:, sl] = o * jax.nn.silu(g_ref[:, sl])


def _hgrn(proj, lbp, gain, B, S):
    nc = S // HG_C
    cb = COLBLK_MAIN

    def spec(k):
        return pl.BlockSpec((HG_C, 512), lambda b, c, k=k: (b * nc + c, cb + k))

    return pl.pallas_call(
        _hgrn_kernel,
        grid=(B, nc),
        in_specs=[spec(0), spec(1), spec(2), spec(3),
                  pl.BlockSpec((3, 512), lambda b, c: (0, 0)),
                  pl.BlockSpec((1, A_DK), lambda b, c: (0, 0))],
        out_specs=pl.BlockSpec((HG_C, 512), lambda b, c: (b * nc + c, 0)),
        out_shape=jax.ShapeDtypeStruct((B * S, 512), F32),
        scratch_shapes=[pltpu.VMEM((A_HEADS, A_DK, A_DK), F32)],
        compiler_params=_cparams(("parallel", "arbitrary")),
        name="hgrn",
    )(proj, proj, proj, proj, lbp, gain)


def _gmlp_kernel(u_ref, v_ref, g_ref, w_ref, b_ref, o_ref, *, nsub):
    r = lax.broadcasted_iota(jnp.int32, (B_BLOCK, B_BLOCK), 0)
    c = lax.broadcasted_iota(jnp.int32, (B_BLOCK, B_BLOCK), 1)
    tril = r >= c
    u = jax.nn.gelu(u_ref[...])
    v = _rms(jax.nn.gelu(v_ref[...]), g_ref[...]).astype(BF16)
    bias = b_ref[...]
    for g in range(B_GROUPS):
        w = jnp.where(tril, w_ref[g], 0.0).astype(BF16)
        cs = slice(g * LANES, (g + 1) * LANES)
        for s in range(nsub):
            rs = slice(s * B_BLOCK, (s + 1) * B_BLOCK)
            mixed = jnp.dot(w, v[rs, cs], preferred_element_type=F32) + bias[:, cs]
            o_ref[rs, cs] = u[rs, cs] * mixed


def _gmlp(proj, gain, w_s, bias_full, tb=512):
    T = proj.shape[0]
    tb = min(tb, T)
    cb = COLBLK_MAIN
    return pl.pallas_call(
        functools.partial(_gmlp_kernel, nsub=tb // B_BLOCK),
        grid=(T // tb,),
        in_specs=[pl.BlockSpec((tb, 512), lambda i: (i, cb + 4)),
                  pl.BlockSpec((tb, 512), lambda i: (i, cb + 5)),
                  pl.BlockSpec((1, 512), lambda i: (0, 0)),
                  pl.BlockSpec((B_GROUPS, B_BLOCK, B_BLOCK), lambda i: (0, 0, 0)),
                  pl.BlockSpec((B_BLOCK, 512), lambda i: (0, 0))],
        out_specs=pl.BlockSpec((tb, 512), lambda i: (i, 0)),
        out_shape=jax.ShapeDtypeStruct((T, 512), F32),
        compiler_params=_cparams(("parallel",)),
        name="gmlp",
    )(proj, proj, gain, w_s, bias_full)


def _dsa_prep_kernel(q_ref, k_ref, v_ref, iq_ref, idx_ref, cc_ref, sc_ref, ci_ref, si_ref,
                     qo_ref, ko_ref, vo_ref, qio_ref, kio_ref):
    cc = cc_ref[...]
    sc = sc_ref[...]
    ci = ci_ref[...]
    si = si_ref[...]
    lane = lax.broadcasted_iota(jnp.int32, ci.shape, 1)
    first_half = (lane % IDX_DIM) < (IDX_DIM // 2)

    def rope_i(x):
        partner = jnp.where(first_half, pltpu.roll(x, LANES - IDX_DIM // 2, 1),
                            pltpu.roll(x, IDX_DIM // 2, 1))
        return x * ci + partner * si

    for h in range(C_HEADS):
        sl = slice(h * LANES, (h + 1) * LANES)
        x = q_ref[:, sl]
        qo_ref[:, sl] = ((x * cc + pltpu.roll(x, C_HEAD_DIM // 2, 1) * sc)
                         * (C_HEAD_DIM ** -0.5)).astype(BF16)
        x = k_ref[:, sl]
        ko_ref[:, sl] = (x * cc + pltpu.roll(x, C_HEAD_DIM // 2, 1) * sc).astype(BF16)
    vo_ref[0] = v_ref[...].T.astype(BF16)
    for j in range(IDX_HEADS // 2):
        r = rope_i(iq_ref[:, j * LANES:(j + 1) * LANES])
        qio_ref[:, (2 * j) * LANES:(2 * j + 1) * LANES] = r.astype(BF16)
        qio_ref[:, (2 * j + 1) * LANES:(2 * j + 2) * LANES] = pltpu.roll(r, IDX_DIM, 1).astype(BF16)
    rk = rope_i(idx_ref[...])
    kio_ref[...] = jnp.where(lane < IDX_DIM, rk, 0.0).astype(BF16)


def _dsa_prep(proj, cos_c, sin_c, cos_i, sin_i):
    T = proj.shape[0]
    tb = DSA_TK
    cb = COLBLK_MAIN

    def pspec(k):
        return pl.BlockSpec((tb, 512), lambda i, k=k: (i, cb + k))

    tspec = pl.BlockSpec((tb, LANES), lambda i: (i, 0))
    return pl.pallas_call(
        _dsa_prep_kernel,
        grid=(T // tb,),
        in_specs=[pspec(6), pspec(7), pspec(8), pspec(9),
                  pl.BlockSpec((tb, LANES), lambda i: (i, COLBLK_IDX)),
                  tspec, tspec, tspec, tspec],
        out_specs=[pl.BlockSpec((tb, 512), lambda i: (i, 0)),
                   pl.BlockSpec((tb, 512), lambda i: (i, 0)),
                   pl.BlockSpec((1, 512, tb), lambda i: (i, 0, 0)),
                   pl.BlockSpec((tb, IDX_HEADS * LANES), lambda i: (i, 0)),
                   pl.BlockSpec((tb, LANES), lambda i: (i, 0))],
        out_shape=[jax.ShapeDtypeStruct((T, 512), BF16),
                   jax.ShapeDtypeStruct((T, 512), BF16),
                   jax.ShapeDtypeStruct((T // tb, 512, tb), BF16),
                   jax.ShapeDtypeStruct((T, IDX_HEADS * LANES), BF16),
                   jax.ShapeDtypeStruct((T, LANES), BF16)],
        compiler_params=_cparams(("parallel",)),
        name="dsa_prep",
    )(proj, proj, proj, proj, proj, cos_c, sin_c, cos_i, sin_i)


DSA_TK = 512


def _sortable(score):
    score = jnp.where(score == 0.0, 0.0, score)
    bits = pltpu.bitcast(score, jnp.int32)
    return bits ^ ((bits >> 31) & 0x7FFFFFFF)


DSA_NACC = 32
I16_MIN = -32768


def _dsa_kernel(q_ref, qi_ref, idx_ref, k_ref, vt_ref, ki_ref, trit_ref, o_ref,
                qidx_scr, key_scr, hi_scr, lo_scr, acc_scr, *, topk):
    TK = DSA_TK
    I16 = jnp.int16
    qb = pl.program_id(1)
    nkb = (qb * Q_BLOCK + Q_BLOCK + TK - 1) // TK
    wscale = (IDX_HEADS ** -0.5) * (IDX_DIM ** -0.5)

    for h in range(IDX_HEADS):
        qidx_scr[h * Q_BLOCK:(h + 1) * Q_BLOCK, :] = qi_ref[:, h * LANES:(h + 1) * LANES]
    iwt = idx_ref[...].T
    w_rows = [iwt[IDX_DIM + h:IDX_DIM + h + 1, :] * wscale for h in range(IDX_HEADS)]
    qchunk = (qb * Q_BLOCK + lax.broadcasted_iota(jnp.int32, (1, Q_BLOCK), 1)) // CHUNK
    row_tk = lax.broadcasted_iota(jnp.int32, (TK, 1), 0)

    def p1(kb, carry):
        off = pl.multiple_of(kb * TK, TK)
        logits = _dot_t(ki_ref[pl.ds(off, TK), :], qidx_scr[...])
        s = None
        for h in range(IDX_HEADS):
            t = jnp.maximum(logits[:, h * Q_BLOCK:(h + 1) * Q_BLOCK], 0.0) * w_rows[h]
            s = t if s is None else s + t
        adm = ((off + row_tk) // CHUNK) <= qchunk
        key = _sortable(jnp.where(adm, s, -jnp.inf))
        key_scr[kb] = key
        hi_scr[kb] = (key >> 16).astype(I16)
        lo_scr[kb] = ((key & 0xFFFF) + I16_MIN).astype(I16)
        return carry

    lax.fori_loop(0, nkb, p1, 0)

    def count16(ref, cand, strict):
        c16 = cand.astype(I16)

        def body(kb, acc):
            t = ref[kb]
            for r in range(TK // DSA_NACC):
                blk = t[r * DSA_NACC:(r + 1) * DSA_NACC]
                hit = (blk > c16) if strict else (blk >= c16)
                acc = acc + jnp.where(hit, jnp.ones((), I16), jnp.zeros((), I16))
            return acc

        acc = lax.fori_loop(0, nkb, body, jnp.zeros((DSA_NACC, LANES), I16))
        return jnp.sum(acc.astype(jnp.int32), axis=0, keepdims=True)

    def bisect16(ref, kneed):
        zero = jnp.zeros((1, Q_BLOCK), jnp.int32)
        base = jnp.where(count16(ref, zero, False) >= kneed, zero, zero + I16_MIN)

        def step(i, base):
            cand = base | (jnp.int32(1) << (14 - i))
            return jnp.where(count16(ref, cand, False) >= kneed, cand, base)

        return lax.fori_loop(0, 15, step, base)

    tau_hi = bisect16(hi_scr, topk)
    cnt_gt_hi = count16(hi_scr, tau_hi, True)
    th16 = tau_hi.astype(I16)

    def keep_group(kb, carry):
        lo_scr[kb] = jnp.where(hi_scr[kb] == th16, lo_scr[kb], jnp.full((), I16_MIN, I16))
        return carry

    lax.fori_loop(0, nkb, keep_group, 0)
    tau_lo = bisect16(lo_scr, topk - cnt_gt_hi)
    cnt_gt = cnt_gt_hi + count16(lo_scr, tau_lo, True)
    tau = (tau_hi << 16) | (tau_lo - I16_MIN)
    need = (topk - cnt_gt).astype(F32)
    key_neg_inf = _sortable(jnp.full((1, 1), -jnp.inf, F32))
    tau_eff = jnp.maximum(tau, key_neg_inf + 1)

    acc_scr[...] = jnp.zeros_like(acc_scr)
    hsl = [slice(h * LANES, (h + 1) * LANES) for h in range(C_HEADS)]

    def p2(kb, carry, tie_order):
        eq_seen, ms, ls = carry
        off = pl.multiple_of(kb * TK, TK)
        kt = key_scr[kb]
        if tie_order:
            eq = kt == tau
            prefix = jnp.dot(trit_ref[...], jnp.where(eq, 1.0, 0.0).astype(BF16),
                             preferred_element_type=F32)
        ss = [_dot_t(k_ref[pl.ds(off, TK), sl], q_ref[:, sl]) for sl in hsl]
        if tie_order:
            over = jnp.where(eq_seen + prefix > need, 1, 0)
            kt = jnp.where(eq, kt - over, kt)
            eq_seen = eq_seen + prefix[TK - 1:TK, :]
        bias = jnp.where(kt >= tau_eff, 0.0, NEG_BIG)
        new_ms, new_ls, alphas, ps = [], [], [], []
        for h in range(C_HEADS):
            s = ss[h] + bias
            m_new = jnp.maximum(ms[h], jnp.max(s, axis=0, keepdims=True))
            alpha = jnp.exp(ms[h] - m_new)
            p = jnp.exp(s - m_new)
            new_ls.append(alpha * ls[h] + jnp.sum(p, axis=0, keepdims=True))
            new_ms.append(m_new)
            alphas.append(alpha)
            ps.append(p.astype(BF16))
        for h in range(C_HEADS):
            acc_scr[h] = alphas[h] * acc_scr[h] + jnp.dot(vt_ref[kb, hsl[h], :], ps[h],
                                                          preferred_element_type=F32)
        return eq_seen, tuple(new_ms), tuple(new_ls)

    row0 = jnp.zeros((1, Q_BLOCK), F32)
    init = (row0,
            tuple(row0 + NEG_BIG for _ in range(C_HEADS)),
            tuple(row0 for _ in range(C_HEADS)))
    cnt_ge = cnt_gt_hi + count16(lo_scr, tau_lo, False)
    _, ms, ls = lax.cond(
        jnp.max(cnt_ge) > topk,
        lambda: lax.fori_loop(0, nkb, functools.partial(p2, tie_order=True), init),
        lambda: lax.fori_loop(0, nkb, functools.partial(p2, tie_order=False), init))
    for h in range(C_HEADS):
        o_ref[:, h * LANES:(h + 1) * LANES] = (acc_scr[h] / ls[h]).T


def _dsa(q_r, qi_r, proj, k_r, v_t, ki_r, trit, B, S):
    nq = S // Q_BLOCK
    topk = min(DSA_TOPK_MAX, S // 4)
    nkt = S // DSA_TK
    assert nkt * DSA_TK == S, "sequence length must be a multiple of the key tile"
    return pl.pallas_call(
        functools.partial(_dsa_kernel, topk=topk),
        grid=(B, nq),
        in_specs=[pl.BlockSpec((Q_BLOCK, 512), lambda b, q: (b * nq + q, 0)),
                  pl.BlockSpec((Q_BLOCK, IDX_HEADS * LANES), lambda b, q: (b * nq + q, 0)),
                  pl.BlockSpec((Q_BLOCK, LANES), lambda b, q: (b * nq + q, COLBLK_IDX)),
                  pl.BlockSpec((S, 512), lambda b, q: (b, 0)),
                  pl.BlockSpec((nkt, 512, DSA_TK), lambda b, q: (b, 0, 0)),
                  pl.BlockSpec((S, LANES), lambda b, q: (b, 0)),
                  pl.BlockSpec((DSA_TK, DSA_TK), lambda b, q: (0, 0))],
        out_specs=pl.BlockSpec((Q_BLOCK, 512), lambda b, q: (b * nq + q, 0)),
        out_shape=jax.ShapeDtypeStruct((B * S, 512), F32),
        scratch_shapes=[pltpu.VMEM((IDX_HEADS * Q_BLOCK, LANES), BF16),
                        pltpu.VMEM((nkt, DSA_TK, Q_BLOCK), jnp.int32),
                        pltpu.VMEM((nkt, DSA_TK, Q_BLOCK), jnp.int16),
                        pltpu.VMEM((nkt, DSA_TK, Q_BLOCK), jnp.int16),
                        pltpu.VMEM((C_HEADS, LANES, Q_BLOCK), F32)],
        compiler_params=_cparams(("parallel", "arbitrary"), VMEM_LIMIT),
        name="dsa",
    )(q_r, qi_r, proj, k_r, v_t, ki_r, trit)


def _merge_kernel(ya_ref, yb_ref, yc_ref, g0_ref, g1_ref, g2_ref, wb_ref, wo_ref, x_ref, o_ref):
    mixed = None
    for y_ref, g_ref, n in ((ya_ref, g0_ref, 0), (yb_ref, g1_ref, 1), (yc_ref, g2_ref, 2)):
        bp = jnp.dot(y_ref[...].astype(BF16), wb_ref[n], preferred_element_type=F32)
        t = jax.nn.sigmoid(g_ref[...]) * bp
        mixed = t if mixed is None else mixed + t
    o_ref[...] = x_ref[...] + jnp.dot(mixed.astype(BF16), wo_ref[...], preferred_element_type=F32)


def _merge(y_a, y_b, y_c, proj, w_branch, w_out, x, tb=256):
    T = x.shape[0]
    tb = min(tb, T)
    yspec = pl.BlockSpec((tb, 512), lambda i: (i, 0))

    def gspec(n):
        return pl.BlockSpec((tb, D_MODEL), lambda i, n=n: (i, n))

    return pl.pallas_call(
        _merge_kernel,
        grid=(T // tb,),
        in_specs=[yspec, yspec, yspec, gspec(0), gspec(1), gspec(2),
                  pl.BlockSpec((N_BRANCH, BRANCH_WIDTH, D_MODEL), lambda i: (0, 0, 0)),
                  pl.BlockSpec((D_MODEL, D_MODEL), lambda i: (0, 0)),
                  pl.BlockSpec((tb, D_MODEL), lambda i: (i, 0))],
        out_specs=pl.BlockSpec((tb, D_MODEL), lambda i: (i, 0)),
        out_shape=jax.ShapeDtypeStruct((T, D_MODEL), F32),
        compiler_params=_cparams(("parallel",), VMEM_LIMIT),
        name="merge",
    )(y_a, y_b, y_c, proj, proj, proj, w_branch, w_out, x)


def _oddeven_merge(lo, hi, r):
    step = r * 2
    if step < hi - lo:
        yield from _oddeven_merge(lo, hi, step)
        yield from _oddeven_merge(lo + r, hi, step)
        yield from [(i, i + r) for i in range(lo + r, hi - r, step)]
    else:
        yield (lo, lo + r)


def _oddeven_sort(lo, hi):
    if hi - lo >= 1:
        mid = lo + (hi - lo) // 2
        yield from _oddeven_sort(lo, mid)
        yield from _oddeven_sort(mid + 1, hi)
        yield from _oddeven_merge(lo, hi, 1)


SUBLANES = 8
NET16 = tuple(_oddeven_sort(0, PEER_TOPK - 1))
NET8 = tuple(_oddeven_sort(0, SUBLANES - 1))
PEER_CAND = tuple((k1, k2) for k1 in range(PEER_TOPK) for k2 in range(PEER_TOPK)
                  if (k1 + 1) * (k2 + 1) <= PEER_TOPK)


def _ce(x, i, j):
    a, b = x[i], x[j]
    x[i] = jnp.maximum(a, b)
    x[j] = jnp.minimum(a, b)


def _bitonic_sort16(m):
    d = PEER_TOPK // 2
    while d >= 1:
        for i in range(PEER_TOPK):
            if (i & d) == 0:
                _ce(m, i, i + d)
        d //= 2


def _merge_sublanes(v, shifts):
    for sh in shifts:
        w = [pltpu.roll(t, sh, 0) for t in v]
        v = [jnp.maximum(v[k], w[PEER_TOPK - 1 - k]) for k in range(PEER_TOPK)]
        _bitonic_sort16(v)
    return v


def _top16_of_rows(s):
    v = [s[SUBLANES * k:SUBLANES * (k + 1), :] for k in range(PEER_N_KEYS // SUBLANES)]
    for i, j in NET16:
        _ce(v, i, j)
    return _merge_sublanes(v, (4, 2, 1))


def _peer_prep_kernel(x_ref, g_ref, wqt_ref, sk_ref, ht_ref, r2_ref, e2_ref, c_ref, cf_ref, hb_scr):
    tb = x_ref.shape[0]
    h2 = _rms(x_ref[...], g_ref[...])
    hb_scr[...] = h2.T.astype(BF16)
    ht_ref[...] = hb_scr[...]
    sub = lax.broadcasted_iota(jnp.int32, (SUBLANES, LANES), 0)
    neg = jnp.full((SUBLANES, LANES), -jnp.inf, F32)

    def head(h, carry):
        hb = hb_scr[...]
        q1 = jnp.dot(wqt_ref[pl.ds(pl.multiple_of(h * 2 * LANES, LANES), LANES), :], hb,
                     preferred_element_type=F32)
        q2 = jnp.dot(wqt_ref[pl.ds(pl.multiple_of(h * 2 * LANES + LANES, LANES), LANES), :], hb,
                     preferred_element_type=F32)
        s1f = jnp.dot(sk_ref[2 * h], q1.astype(BF16), preferred_element_type=F32)
        s2f = jnp.dot(sk_ref[2 * h + 1], q2.astype(BF16), preferred_element_type=F32)
        for lt in range(tb // LANES):
            ls = slice(lt * LANES, (lt + 1) * LANES)
            s1 = s1f[:, ls]
            s2 = s2f[:, ls]
            a = _top16_of_rows(s1)
            b = _top16_of_rows(s2)
            packed = []
            for v in range((len(PEER_CAND) + SUBLANES - 1) // SUBLANES):
                p = neg
                for s in range(SUBLANES):
                    c = SUBLANES * v + s
                    if c < len(PEER_CAND):
                        k1, k2 = PEER_CAND[c]
                        p = jnp.where(sub == s, a[k1] + b[k2], p)
                packed.append(p)
            while len(packed) < SUBLANES:
                packed.append(neg)
            for i, j in NET8:
                _ce(packed, i, j)
            w = [pltpu.roll(t, 4, 0) for t in packed]
            m = packed + w[::-1]
            _bitonic_sort16(m)
            m = _merge_sublanes(m, (2, 1))
            thr = m[PEER_TOPK - 1][0:1, :]
            zsum = m[0] - m[0] + 1.0
            for k in range(1, PEER_TOPK):
                zsum = zsum + jnp.exp(m[k] - m[0])
            zinv = 1.0 / zsum[0:1, :]
            cnt = jnp.zeros((PEER_N_KEYS, LANES), F32)
            rank = jnp.zeros((PEER_N_KEYS, LANES), F32)
            for k in range(PEER_TOPK):
                bk = b[k][0:1, :]
                cnt = cnt + jnp.where(s1 + bk >= thr, 1.0, 0.0)
                rank = rank + jnp.where(bk > s2, 1.0, 0.0)
            r2_ref[h, :, ls] = rank.astype(r2_ref.dtype)
            e2_ref[h, :, ls] = jnp.exp(s2 - b[0][0:1, :]).astype(e2_ref.dtype)
            c_ref[h, :, ls] = cnt
            cf_ref[h, :, ls] = jnp.exp(s1 - a[0][0:1, :]) * zinv
        return carry

    lax.fori_loop(0, PEER_HEADS, head, 0)


def _peer_prep(x, gain, w_qt, sub_keys, tb=256):
    T = x.shape[0]
    tb = min(tb, T)
    aux_spec = pl.BlockSpec((PEER_HEADS, PEER_N_KEYS, tb), lambda i: (0, 0, i))
    aux_f32 = jax.ShapeDtypeStruct((PEER_HEADS, PEER_N_KEYS, T), F32)
    aux_b16 = jax.ShapeDtypeStruct((PEER_HEADS, PEER_N_KEYS, T), BF16)
    return pl.pallas_call(
        _peer_prep_kernel,
        grid=(T // tb,),
        in_specs=[pl.BlockSpec((tb, D_MODEL), lambda i: (i, 0)),
                  pl.BlockSpec((1, D_MODEL), lambda i: (0, 0)),
                  pl.BlockSpec((2 * PEER_HEADS * LANES, D_MODEL), lambda i: (0, 0)),
                  pl.BlockSpec((2 * PEER_HEADS, PEER_N_KEYS, LANES), lambda i: (0, 0, 0))],
        out_specs=[pl.BlockSpec((D_MODEL, tb), lambda i: (0, i)),
                   aux_spec, aux_spec, aux_spec, aux_spec],
        out_shape=[jax.ShapeDtypeStruct((D_MODEL, T), BF16),
                   aux_b16, aux_b16, aux_f32, aux_f32],
        scratch_shapes=[pltpu.VMEM((D_MODEL, tb), BF16)],
        compiler_params=_cparams(("parallel",), VMEM_LIMIT),
        name="peer_prep",
    )(x, gain, w_qt, sub_keys)


PEER_ET = 2048
PEER_MM = 256


GELU_K1 = -2.0 * float(np.sqrt(2.0 / np.pi)) * float(np.log2(np.e))
GELU_K3 = GELU_K1 * 0.044715


def _peer_dense_kernel(x_ref, ht_ref, r2_ref, e2_ref, c_ref, cf_ref, u_ref, vt_ref, o_ref,
                       acc_scr, a_scr, ga_scr):
    et = pl.program_id(1)
    n_sub = PEER_ET // PEER_N_KEYS

    @pl.when(et == 0)
    def _():
        acc_scr[...] = jnp.zeros_like(acc_scr)

    hb = ht_ref[...]
    n_pair = PEER_ET // PEER_MM
    per_mm = PEER_MM // PEER_N_KEYS
    for pr in range(n_pair):
        ps = slice(pr * PEER_MM, (pr + 1) * PEER_MM)
        a_scr[ps, :] = jnp.dot(u_ref[ps, :], hb, preferred_element_type=F32)
    for pr in range(n_pair):
        ps = slice(pr * PEER_MM, (pr + 1) * PEER_MM)
        for half in range(per_mm):
            rs = slice(pr * PEER_MM + half * PEER_N_KEYS, pr * PEER_MM + (half + 1) * PEER_N_KEYS)
            i = et * n_sub + pr * per_mm + half
            a = a_scr[rs, :].astype(BF16)
            act = a / (1.0 + jnp.exp2(a * (GELU_K1 + GELU_K3 * (a * a))))
            g = None
            for h in range(PEER_HEADS):
                c_row = c_ref[h, pl.ds(i, 1), :].astype(r2_ref.dtype)
                cf_row = cf_ref[h, pl.ds(i, 1), :].astype(e2_ref.dtype)
                t = jnp.where(r2_ref[h] < c_row, e2_ref[h] * cf_row, jnp.zeros((), e2_ref.dtype))
                g = t if g is None else g + t
            ga_scr[rs, :] = g.astype(BF16) * act
    acc_scr[...] += jnp.dot(vt_ref[...], ga_scr[...], preferred_element_type=F32)

    @pl.when(et == pl.num_programs(1) - 1)
    def _():
        o_ref[...] = x_ref[...] + acc_scr[...].T


def _peer_dense(x, ht, r2, e2, cnt, coef, u_tab, vt_tab, tb=512):
    T = x.shape[0]
    tb = min(tb, T)
    n_exp = u_tab.shape[0]
    aux_spec = pl.BlockSpec((PEER_HEADS, PEER_N_KEYS, tb), lambda i, e: (0, 0, i))
    return pl.pallas_call(
        _peer_dense_kernel,
        grid=(T // tb, n_exp // PEER_ET),
        in_specs=[pl.BlockSpec((tb, D_MODEL), lambda i, e: (i, 0)),
                  pl.BlockSpec((D_MODEL, tb), lambda i, e: (0, i)),
                  aux_spec, aux_spec, aux_spec, aux_spec,
                  pl.BlockSpec((PEER_ET, D_MODEL), lambda i, e: (e, 0)),
                  pl.BlockSpec((D_MODEL, PEER_ET), lambda i, e: (0, e))],
        out_specs=pl.BlockSpec((tb, D_MODEL), lambda i, e: (i, 0)),
        out_shape=jax.ShapeDtypeStruct((T, D_MODEL), F32),
        scratch_shapes=[pltpu.VMEM((D_MODEL, tb), F32),
                        pltpu.VMEM((PEER_ET, tb), F32),
                        pltpu.VMEM((PEER_ET, tb), BF16)],
        compiler_params=_cparams(("parallel", "arbitrary"), VMEM_LIMIT),
        name="peer_dense",
    )(x, ht, r2, e2, cnt, coef, u_tab, vt_tab)


def _final_kernel(x_ref, g_ref, o_ref):
    o_ref[...] = _rms(x_ref[...], g_ref[...])


def _final_norm(x, gain, tb=1024):
    T = x.shape[0]
    tb = min(tb, T)
    return pl.pallas_call(
        _final_kernel,
        grid=(T // tb,),
        in_specs=[pl.BlockSpec((tb, D_MODEL), lambda i: (i, 0)),
                  pl.BlockSpec((1, D_MODEL), lambda i: (0, 0))],
        out_specs=pl.BlockSpec((tb, D_MODEL), lambda i: (i, 0)),
        out_shape=jax.ShapeDtypeStruct((T, D_MODEL), F32),
        compiler_params=_cparams(("parallel",)),
        name="final_norm",
    )(x, gain)


def _rope_tables(positions):
    pos = positions.astype(F32).reshape(-1, 1)

    def tables(dim):
        half = dim // 2
        inv = ROPE_THETA ** (-jnp.arange(half, dtype=F32) / half)
        ang = pos * inv
        cos = jnp.cos(ang)
        sin = jnp.sin(ang)
        reps = LANES // dim
        return (jnp.tile(jnp.concatenate([cos, cos], axis=-1), (1, reps)),
                jnp.tile(jnp.concatenate([-sin, sin], axis=-1), (1, reps)))

    cos_c, sin_c = tables(C_HEAD_DIM)
    cos_i, sin_i = tables(IDX_DIM)
    return cos_c, sin_c, cos_i, sin_i


def _pack_w_in(w):
    n_idx = IDX_DIM + IDX_HEADS
    main = w[:, :N_MAIN]
    idx = w[:, N_MAIN:N_MAIN + n_idx]
    gates = w[:, N_MAIN + n_idx:]
    pad = jnp.zeros((w.shape[0], LANES - n_idx), w.dtype)
    return jnp.concatenate([gates, main, idx, pad], axis=1).astype(BF16)


def _layer(x, l, B, S, tabs, tri, lbp_all, p):
    proj = _inproj(x, p["norm1_gain"][l][None, :], _pack_w_in(p["w_in"][l]))
    y_a = _hgrn(proj, lbp_all[l], p["hgrn_norm_gain"][l][None, :], B, S)
    bias_full = jnp.repeat(p["gmlp_b_s"][l].T, B_BLOCK, axis=1)
    y_b = _gmlp(proj, p["gmlp_norm_gain"][l][None, :], p["gmlp_w_s"][l], bias_full)
    q_r, k_r, v_t, qi_r, ki_r = _dsa_prep(proj, *tabs)
    y_c = _dsa(q_r, qi_r, proj, k_r, v_t, ki_r, tri, B, S)
    x = _merge(y_a, y_b, y_c, proj, p["w_branch"][l].astype(BF16), p["w_out"][l].astype(BF16), x)
    sk = p["peer_sub_keys"][l].reshape(2 * PEER_HEADS, PEER_N_KEYS, LANES).astype(BF16)
    ht, r2, e2, cnt, coef = _peer_prep(x, p["norm2_gain"][l][None, :],
                                       p["peer_w_q"][l].T.astype(BF16), sk)
    return _peer_dense(x, ht, r2, e2, cnt, coef, p["peer_u"][l].astype(BF16),
                       p["peer_v"][l].T.astype(BF16))


def kernel(x, positions, norm1_gain, w_in, hgrn_lb_logits, hgrn_norm_gain, gmlp_norm_gain,
           gmlp_w_s, gmlp_b_s, w_branch, w_out, norm2_gain, peer_w_q, peer_sub_keys,
           peer_u, peer_v, final_gain):
    B, S, _ = x.shape
    depth = w_in.shape[0]
    p = dict(norm1_gain=norm1_gain, w_in=w_in, hgrn_norm_gain=hgrn_norm_gain,
             gmlp_norm_gain=gmlp_norm_gain, gmlp_w_s=gmlp_w_s, gmlp_b_s=gmlp_b_s,
             w_branch=w_branch, w_out=w_out, norm2_gain=norm2_gain, peer_w_q=peer_w_q,
             peer_sub_keys=peer_sub_keys, peer_u=peer_u, peer_v=peer_v)
    sm = jax.nn.softmax(hgrn_lb_logits.astype(F32), axis=0)
    cs = jnp.cumsum(sm, axis=0)
    lb = cs - cs[0:1]
    lbp_all = jnp.stack([jnp.log(lb), jnp.log1p(-lb), 1.0 - lb], axis=1)
    tabs = _rope_tables(positions)
    r = lax.broadcasted_iota(jnp.int32, (DSA_TK, DSA_TK), 0)
    c = lax.broadcasted_iota(jnp.int32, (DSA_TK, DSA_TK), 1)
    tri = (r >= c).astype(BF16)
    xt = x.reshape(B * S, D_MODEL)
    for l in range(depth):
        xt = _layer(xt, l, B, S, tabs, tri, lbp_all, p)
    return _final_norm(xt, final_gain[None, :]).reshape(B, S, D_MODEL)
```

```python
import functools

import numpy as np
import jax
import jax.numpy as jnp
from jax import lax
from jax.experimental import pallas as pl
from jax.experimental.pallas import tpu as pltpu

F32 = jnp.float32
BF16 = jnp.bfloat16

D_MODEL = 1024
CHUNK = 64
EPS = 1e-6
ROPE_THETA = 10000.0
A_HEADS = 4
A_DK = 128
B_GROUPS = 4
B_BLOCK = 128
C_HEADS = 4
C_HEAD_DIM = 128
IDX_HEADS = 8
IDX_DIM = 64
DSA_TOPK_MAX = 256
Q_BLOCK = 128
N_BRANCH = 3
BRANCH_WIDTH = 512
PEER_HEADS = 8
PEER_N_KEYS = 128
PEER_TOPK = 16

LANES = 128
VMEM_LIMIT = 56 * 1024 * 1024

N_MAIN = 10 * 512
N_PACKED = N_BRANCH * D_MODEL + N_MAIN + LANES
COLBLK_MAIN = (N_BRANCH * D_MODEL) // 512
COLBLK_IDX = (N_BRANCH * D_MODEL + N_MAIN) // LANES

NEG_BIG = -1e30
INT_MIN = -2147483648


def _cparams(sem, vmem=None):
    return pltpu.CompilerParams(dimension_semantics=sem, vmem_limit_bytes=vmem)


def _rms(x, gain):
    return x * lax.rsqrt(jnp.mean(x * x, axis=-1, keepdims=True) + EPS) * gain


def _dot_t(a, b):
    return lax.dot_general(a, b, (((1,), (1,)), ((), ())), preferred_element_type=F32)


def _inproj_kernel(x_ref, g_ref, w_ref, o_ref, h_scr):
    @pl.when(pl.program_id(1) == 0)
    def _():
        h_scr[...] = _rms(x_ref[...], g_ref[...]).astype(BF16)

    o_ref[...] = jnp.dot(h_scr[...], w_ref[...], preferred_element_type=F32)


def _inproj(x, gain, w_packed, tm=1024, tn=1664):
    T = x.shape[0]
    tm = min(tm, T)
    return pl.pallas_call(
        _inproj_kernel,
        grid=(T // tm, N_PACKED // tn),
        in_specs=[pl.BlockSpec((tm, D_MODEL), lambda i, j: (i, 0)),
                  pl.BlockSpec((1, D_MODEL), lambda i, j: (0, 0)),
                  pl.BlockSpec((D_MODEL, tn), lambda i, j: (0, j))],
        out_specs=pl.BlockSpec((tm, tn), lambda i, j: (i, j)),
        out_shape=jax.ShapeDtypeStruct((T, N_PACKED), F32),
        scratch_shapes=[pltpu.VMEM((tm, D_MODEL), BF16)],
        compiler_params=_cparams(("parallel", "arbitrary"), VMEM_LIMIT),
        name="inproj",
    )(x, gain, w_packed)


HG_C = 128
HG_SUB = 8


def _hgrn_kernel(q_ref, f_ref, i_ref, g_ref, lbp_ref, gain_ref, o_ref, st_scr):
    @pl.when(pl.program_id(1) == 0)
    def _():
        st_scr[...] = jnp.zeros_like(st_scr)

    C = HG_C
    row = lax.broadcasted_iota(jnp.int32, (C, 1), 0)
    rowm = lax.broadcasted_iota(jnp.int32, (C, C), 0)
    colm = lax.broadcasted_iota(jnp.int32, (C, C), 1)
    gain = gain_ref[...]

    for h in range(A_HEADS):
        sl = slice(h * A_DK, (h + 1) * A_DK)
        z = f_ref[:, sl]
        loglb = lbp_ref[0:1, sl]
        log1mlb = lbp_ref[1:2, sl]
        omlb = lbp_ref[2:3, sl]
        lsig = -(jnp.maximum(-z, 0.0) + jnp.log1p(jnp.exp(-jnp.abs(z))))
        cterm = log1mlb + lsig
        logf = jnp.maximum(loglb, cterm) + jnp.log1p(jnp.exp(-jnp.abs(loglb - cterm)))
        key = omlb * jax.nn.sigmoid(-z)
        qf = jax.nn.silu(q_ref[:, sl]) * (A_DK ** -0.5)
        vf = i_ref[:, sl]

        b = logf
        sh = 1
        while sh < C:
            b = b + jnp.where(row >= sh, pltpu.roll(b, sh, 0), 0.0)
            sh *= 2

        attn = jnp.zeros((C, C), F32)
        m = C // 2
        while m >= HG_SUB:
            ref = jnp.concatenate(
                [jnp.broadcast_to(b[j * 2 * m + m - 1:j * 2 * m + m, :], (2 * m, A_DK))
                 for j in range(C // (2 * m))], axis=0)
            qt = qf * jnp.exp(jnp.minimum(b - ref, 0.0))
            kt = key * jnp.exp(jnp.minimum(ref - b, 0.0))
            a = _dot_t(qt.astype(BF16), kt.astype(BF16))
            msk = ((rowm // (2 * m)) == (colm // (2 * m))) & ((rowm % (2 * m)) >= m) & ((colm % (2 * m)) < m)
            attn = attn + jnp.where(msk, a, 0.0)
            m //= 2

        nsub = C // HG_SUB
        for j in range(HG_SUB):
            kj = jnp.concatenate(
                [jnp.broadcast_to(key[i * HG_SUB + j:i * HG_SUB + j + 1, :], (HG_SUB, A_DK))
                 for i in range(nsub)], axis=0)
            bj = jnp.concatenate(
                [jnp.broadcast_to(b[i * HG_SUB + j:i * HG_SUB + j + 1, :], (HG_SUB, A_DK))
                 for i in range(nsub)], axis=0)
            d = jnp.exp(jnp.minimum(b - bj, 0.0)) * qf * kj
            red = jnp.sum(d, axis=-1, keepdims=True)
            hit = (colm == (rowm // HG_SUB) * HG_SUB + j) & ((rowm % HG_SUB) >= j)
            attn = jnp.where(hit, red, attn)

        st = st_scr[h]
        o = jnp.dot(attn.astype(BF16), vf.astype(BF16), preferred_element_type=F32)
        o = o + _dot_t((qf * jnp.exp(b)).astype(BF16), st.astype(BF16))
        blast = b[C - 1:C, :]
        kdec = key * jnp.exp(blast - b)
        upd = lax.dot_general(vf.astype(BF16), kdec.astype(BF16), (((0,), (0,)), ((), ())),
                              preferred_element_type=F32)
        st_scr[h] = st * jnp.exp(blast) + upd

        o = _rms(o, gain)
        o_ref[:, sl] = o * jax.nn.silu(g_ref[:, sl])


def _hgrn(proj, lbp, gain, B, S):
    nc = S // HG_C
    cb = COLBLK_MAIN

    def spec(k):
        return pl.BlockSpec((HG_C, 512), lambda b, c, k=k: (b * nc + c, cb + k))

    return pl.pallas_call(
        _hgrn_kernel,
        grid=(B, nc),
        in_specs=[spec(0), spec(1), spec(2), spec(3),
                  pl.BlockSpec((3, 512), lambda b, c: (0, 0)),
                  pl.BlockSpec((1, A_DK), lambda b, c: (0, 0))],
        out_specs=pl.BlockSpec((HG_C, 512), lambda b, c: (b * nc + c, 0)),
        out_shape=jax.ShapeDtypeStruct((B * S, 512), F32),
        scratch_shapes=[pltpu.VMEM((A_HEADS, A_DK, A_DK), F32)],
        compiler_params=_cparams(("parallel", "arbitrary")),
        name="hgrn",
    )(proj, proj, proj, proj, lbp, gain)


def _gmlp_kernel(u_ref, v_ref, g_ref, w_ref, b_ref, o_ref, *, nsub):
    r = lax.broadcasted_iota(jnp.int32, (B_BLOCK, B_BLOCK), 0)
    c = lax.broadcasted_iota(jnp.int32, (B_BLOCK, B_BLOCK), 1)
    tril = r >= c
    u = jax.nn.gelu(u_ref[...])
    v = _rms(jax.nn.gelu(v_ref[...]), g_ref[...]).astype(BF16)
    bias = b_ref[...]
    for g in range(B_GROUPS):
        w = jnp.where(tril, w_ref[g], 0.0).astype(BF16)
        cs = slice(g * LANES, (g + 1) * LANES)
        for s in range(nsub):
            rs = slice(s * B_BLOCK, (s + 1) * B_BLOCK)
            mixed = jnp.dot(w, v[rs, cs], preferred_element_type=F32) + bias[:, cs]
            o_ref[rs, cs] = u[rs, cs] * mixed


def _gmlp(proj, gain, w_s, bias_full, tb=512):
    T = proj.shape[0]
    tb = min(tb, T)
    cb = COLBLK_MAIN
    return pl.pallas_call(
        functools.partial(_gmlp_kernel, nsub=tb // B_BLOCK),
        grid=(T // tb,),
        in_specs=[pl.BlockSpec((tb, 512), lambda i: (i, cb + 4)),
                  pl.BlockSpec((tb, 512), lambda i: (i, cb + 5)),
                  pl.BlockSpec((1, 512), lambda i: (0, 0)),
                  pl.BlockSpec((B_GROUPS, B_BLOCK, B_BLOCK), lambda i: (0, 0, 0)),
                  pl.BlockSpec((B_BLOCK, 512), lambda i: (0, 0))],
        out_specs=pl.BlockSpec((tb, 512), lambda i: (i, 0)),
        out_shape=jax.ShapeDtypeStruct((T, 512), F32),
        compiler_params=_cparams(("parallel",)),
        name="gmlp",
    )(proj, proj, gain, w_s, bias_full)


def _dsa_prep_kernel(q_ref, k_ref, v_ref, iq_ref, idx_ref, cc_ref, sc_ref, ci_ref, si_ref,
                     qo_ref, ko_ref, vo_ref, qio_ref, kio_ref):
    cc = cc_ref[...]
    sc = sc_ref[...]
    ci = ci_ref[...]
    si = si_ref[...]
    lane = lax.broadcasted_iota(jnp.int32, ci.shape, 1)
    first_half = (lane % IDX_DIM) < (IDX_DIM // 2)

    def rope_i(x):
        partner = jnp.where(first_half, pltpu.roll(x, LANES - IDX_DIM // 2, 1),
                            pltpu.roll(x, IDX_DIM // 2, 1))
        return x * ci + partner * si

    for h in range(C_HEADS):
        sl = slice(h * LANES, (h + 1) * LANES)
        x = q_ref[:, sl]
        qo_ref[:, sl] = ((x * cc + pltpu.roll(x, C_HEAD_DIM // 2, 1) * sc)
                         * (C_HEAD_DIM ** -0.5)).astype(BF16)
        x = k_ref[:, sl]
        ko_ref[:, sl] = (x * cc + pltpu.roll(x, C_HEAD_DIM // 2, 1) * sc).astype(BF16)
    vo_ref[0] = v_ref[...].T.astype(BF16)
    for j in range(IDX_HEADS // 2):
        r = rope_i(iq_ref[:, j * LANES:(j + 1) * LANES])
        qio_ref[:, (2 * j) * LANES:(2 * j + 1) * LANES] = r.astype(BF16)
        qio_ref[:, (2 * j + 1) * LANES:(2 * j + 2) * LANES] = pltpu.roll(r, IDX_DIM, 1).astype(BF16)
    rk = rope_i(idx_ref[...])
    kio_ref[...] = jnp.where(lane < IDX_DIM, rk, 0.0).astype(BF16)


def _dsa_prep(proj, cos_c, sin_c, cos_i, sin_i):
    T = proj.shape[0]
    tb = DSA_TK
    cb = COLBLK_MAIN

    def pspec(k):
        return pl.BlockSpec((tb, 512), lambda i, k=k: (i, cb + k))

    tspec = pl.BlockSpec((tb, LANES), lambda i: (i, 0))
    return pl.pallas_call(
        _dsa_prep_kernel,
        grid=(T // tb,),
        in_specs=[pspec(6), pspec(7), pspec(8), pspec(9),
                  pl.BlockSpec((tb, LANES), lambda i: (i, COLBLK_IDX)),
                  tspec, tspec, tspec, tspec],
        out_specs=[pl.BlockSpec((tb, 512), lambda i: (i, 0)),
                   pl.BlockSpec((tb, 512), lambda i: (i, 0)),
                   pl.BlockSpec((1, 512, tb), lambda i: (i, 0, 0)),
                   pl.BlockSpec((tb, IDX_HEADS * LANES), lambda i: (i, 0)),
                   pl.BlockSpec((tb, LANES), lambda i: (i, 0))],
        out_shape=[jax.ShapeDtypeStruct((T, 512), BF16),
                   jax.ShapeDtypeStruct((T, 512), BF16),
                   jax.ShapeDtypeStruct((T // tb, 512, tb), BF16),
                   jax.ShapeDtypeStruct((T, IDX_HEADS * LANES), BF16),
                   jax.ShapeDtypeStruct((T, LANES), BF16)],
        compiler_params=_cparams(("parallel",)),
        name="dsa_prep",
    )(proj, proj, proj, proj, proj, cos_c, sin_c, cos_i, sin_i)


DSA_TK = 1024


def _sortable(score):
    score = jnp.where(score == 0.0, 0.0, score)
    bits = pltpu.bitcast(score, jnp.int32)
    return bits ^ ((bits >> 31) & 0x7FFFFFFF)


DSA_NACC = 32
I16_MIN = -32768


def _dsa_kernel(q_ref, qi_ref, idx_ref, k_ref, vt_ref, ki_ref, trit_ref, o_ref,
                qidx_scr, key_scr, hi_scr, lo_scr, acc_scr, *, topk):
    TK = DSA_TK
    I16 = jnp.int16
    qb = pl.program_id(1)
    nkb = (qb * Q_BLOCK + Q_BLOCK + TK - 1) // TK
    wscale = (IDX_HEADS ** -0.5) * (IDX_DIM ** -0.5)

    for h in range(IDX_HEADS):
        qidx_scr[h * Q_BLOCK:(h + 1) * Q_BLOCK, :] = qi_ref[:, h * LANES:(h + 1) * LANES]
    iwt = idx_ref[...].T
    w_rows = [iwt[IDX_DIM + h:IDX_DIM + h + 1, :] * wscale for h in range(IDX_HEADS)]
    qchunk = (qb * Q_BLOCK + lax.broadcasted_iota(jnp.int32, (1, Q_BLOCK), 1)) // CHUNK
    row_tk = lax.broadcasted_iota(jnp.int32, (TK, 1), 0)

    def p1(kb, carry):
        off = pl.multiple_of(kb * TK, TK)
        logits = _dot_t(ki_ref[pl.ds(off, TK), :], qidx_scr[...])
        s = None
        for h in range(IDX_HEADS):
            t = jnp.maximum(logits[:, h * Q_BLOCK:(h + 1) * Q_BLOCK], 0.0) * w_rows[h]
            s = t if s is None else s + t
        adm = ((off + row_tk) // CHUNK) <= qchunk
        key = _sortable(jnp.where(adm, s, -jnp.inf))
        key_scr[kb] = key
        hi_scr[kb] = (key >> 16).astype(I16)
        lo_scr[kb] = ((key & 0xFFFF) + I16_MIN).astype(I16)
        return carry

    lax.fori_loop(0, nkb, p1, 0)

    def count16(ref, cand, strict):
        c16 = cand.astype(I16)

        def body(kb, acc):
            t = ref[kb]
            for r in range(TK // DSA_NACC):
                blk = t[r * DSA_NACC:(r + 1) * DSA_NACC]
                hit = (blk > c16) if strict else (blk >= c16)
                acc = acc + jnp.where(hit, jnp.ones((), I16), jnp.zeros((), I16))
            return acc

        acc = lax.fori_loop(0, nkb, body, jnp.zeros((DSA_NACC, LANES), I16))
        return jnp.sum(acc.astype(jnp.int32), axis=0, keepdims=True)

    def bisect16(ref, kneed):
        zero = jnp.zeros((1, Q_BLOCK), jnp.int32)
        base = jnp.where(count16(ref, zero, False) >= kneed, zero, zero + I16_MIN)

        def step(i, base):
            cand = base | (jnp.int32(1) << (14 - i))
            return jnp.where(count16(ref, cand, False) >= kneed, cand, base)

        return lax.fori_loop(0, 15, step, base)

    tau_hi = bisect16(hi_scr, topk)
    cnt_gt_hi = count16(hi_scr, tau_hi, True)
    th16 = tau_hi.astype(I16)

    def keep_group(kb, carry):
        lo_scr[kb] = jnp.where(hi_scr[kb] == th16, lo_scr[kb], jnp.full((), I16_MIN, I16))
        return carry

    lax.fori_loop(0, nkb, keep_group, 0)
    tau_lo = bisect16(lo_scr, topk - cnt_gt_hi)
    cnt_gt = cnt_gt_hi + count16(lo_scr, tau_lo, True)
    tau = (tau_hi << 16) | (tau_lo - I16_MIN)
    need = (topk - cnt_gt).astype(F32)
    key_neg_inf = _sortable(jnp.full((1, 1), -jnp.inf, F32))
    tau_eff = jnp.maximum(tau, key_neg_inf + 1)

    acc_scr[...] = jnp.zeros_like(acc_scr)
    hsl = [slice(h * LANES, (h + 1) * LANES) for h in range(C_HEADS)]

    def p2(kb, carry, tie_order):
        eq_seen, ms, ls = carry
        off = pl.multiple_of(kb * TK, TK)
        kt = key_scr[kb]
        if tie_order:
            eq = kt == tau
            prefix = jnp.dot(trit_ref[...], jnp.where(eq, 1.0, 0.0).astype(BF16),
                             preferred_element_type=F32)
        ss = [_dot_t(k_ref[pl.ds(off, TK), sl], q_ref[:, sl]) for sl in hsl]
        if tie_order:
            over = jnp.where(eq_seen + prefix > need, 1, 0)
            kt = jnp.where(eq, kt - over, kt)
            eq_seen = eq_seen + prefix[TK - 1:TK, :]
        bias = jnp.where(kt >= tau_eff, 0.0, NEG_BIG)
        new_ms, new_ls, alphas, ps = [], [], [], []
        for h in range(C_HEADS):
            s = ss[h] + bias
            m_new = jnp.maximum(ms[h], jnp.max(s, axis=0, keepdims=True))
            alpha = jnp.exp(ms[h] - m_new)
            p = jnp.exp(s - m_new)
            new_ls.append(alpha * ls[h] + jnp.sum(p, axis=0, keepdims=True))
            new_ms.append(m_new)
            alphas.append(alpha)
            ps.append(p.astype(BF16))
        for h in range(C_HEADS):
            acc_scr[h] = alphas[h] * acc_scr[h] + jnp.dot(vt_ref[kb, hsl[h], :], ps[h],
                                                          preferred_element_type=F32)
        return eq_seen, tuple(new_ms), tuple(new_ls)

    row0 = jnp.zeros((1, Q_BLOCK), F32)
    init = (row0,
            tuple(row0 + NEG_BIG for _ in range(C_HEADS)),
            tuple(row0 for _ in range(C_HEADS)))
    cnt_ge = cnt_gt_hi + count16(lo_scr, tau_lo, False)
    _, ms, ls = lax.cond(
        jnp.max(cnt_ge) > topk,
        lambda: lax.fori_loop(0, nkb, functools.partial(p2, tie_order=True), init),
        lambda: lax.fori_loop(0, nkb, functools.partial(p2, tie_order=False), init))
    for h in range(C_HEADS):
        o_ref[:, h * LANES:(h + 1) * LANES] = (acc_scr[h] / ls[h]).T


def _dsa(q_r, qi_r, proj, k_r, v_t, ki_r, trit, B, S):
    nq = S // Q_BLOCK
    topk = min(DSA_TOPK_MAX, S // 4)
    nkt = S // DSA_TK
    assert nkt * DSA_TK == S, "sequence length must be a multiple of the key tile"
    return pl.pallas_call(
        functools.partial(_dsa_kernel, topk=topk),
        grid=(B, nq),
        in_specs=[pl.BlockSpec((Q_BLOCK, 512), lambda b, q: (b * nq + q, 0)),
                  pl.BlockSpec((Q_BLOCK, IDX_HEADS * LANES), lambda b, q: (b * nq + q, 0)),
                  pl.BlockSpec((Q_BLOCK, LANES), lambda b, q: (b * nq + q, COLBLK_IDX)),
                  pl.BlockSpec((S, 512), lambda b, q: (b, 0)),
                  pl.BlockSpec((nkt, 512, DSA_TK), lambda b, q: (b, 0, 0)),
                  pl.BlockSpec((S, LANES), lambda b, q: (b, 0)),
                  pl.BlockSpec((DSA_TK, DSA_TK), lambda b, q: (0, 0))],
        out_specs=pl.BlockSpec((Q_BLOCK, 512), lambda b, q: (b * nq + q, 0)),
        out_shape=jax.ShapeDtypeStruct((B * S, 512), F32),
        scratch_shapes=[pltpu.VMEM((IDX_HEADS * Q_BLOCK, LANES), BF16),
                        pltpu.VMEM((nkt, DSA_TK, Q_BLOCK), jnp.int32),
                        pltpu.VMEM((nkt, DSA_TK, Q_BLOCK), jnp.int16),
                        pltpu.VMEM((nkt, DSA_TK, Q_BLOCK), jnp.int16),
                        pltpu.VMEM((C_HEADS, LANES, Q_BLOCK), F32)],
        compiler_params=_cparams(("parallel", "arbitrary"), VMEM_LIMIT),
        name="dsa",
    )(q_r, qi_r, proj, k_r, v_t, ki_r, trit)


def _merge_kernel(ya_ref, yb_ref, yc_ref, g0_ref, g1_ref, g2_ref, wb_ref, wo_ref, x_ref, o_ref):
    mixed = None
    for y_ref, g_ref, n in ((ya_ref, g0_ref, 0), (yb_ref, g1_ref, 1), (yc_ref, g2_ref, 2)):
        bp = jnp.dot(y_ref[...].astype(BF16), wb_ref[n], preferred_element_type=F32)
        t = jax.nn.sigmoid(g_ref[...]) * bp
        mixed = t if mixed is None else mixed + t
    o_ref[...] = x_ref[...] + jnp.dot(mixed.astype(BF16), wo_ref[...], preferred_element_type=F32)


def _merge(y_a, y_b, y_c, proj, w_branch, w_out, x, tb=256):
    T = x.shape[0]
    tb = min(tb, T)
    yspec = pl.BlockSpec((tb, 512), lambda i: (i, 0))

    def gspec(n):
        return pl.BlockSpec((tb, D_MODEL), lambda i, n=n: (i, n))

    return pl.pallas_call(
        _merge_kernel,
        grid=(T // tb,),
        in_specs=[yspec, yspec, yspec, gspec(0), gspec(1), gspec(2),
                  pl.BlockSpec((N_BRANCH, BRANCH_WIDTH, D_MODEL), lambda i: (0, 0, 0)),
                  pl.BlockSpec((D_MODEL, D_MODEL), lambda i: (0, 0)),
                  pl.BlockSpec((tb, D_MODEL), lambda i: (i, 0))],
        out_specs=pl.BlockSpec((tb, D_MODEL), lambda i: (i, 0)),
        out_shape=jax.ShapeDtypeStruct((T, D_MODEL), F32),
        compiler_params=_cparams(("parallel",), VMEM_LIMIT),
        name="merge",
    )(y_a, y_b, y_c, proj, proj, proj, w_branch, w_out, x)


def _oddeven_merge(lo, hi, r):
    step = r * 2
    if step < hi - lo:
        yield from _oddeven_merge(lo, hi, step)
        yield from _oddeven_merge(lo + r, hi, step)
        yield from [(i, i + r) for i in range(lo + r, hi - r, step)]
    else:
        yield (lo, lo + r)


def _oddeven_sort(lo, hi):
    if hi - lo >= 1:
        mid = lo + (hi - lo) // 2
        yield from _oddeven_sort(lo, mid)
        yield from _oddeven_sort(mid + 1, hi)
        yield from _oddeven_merge(lo, hi, 1)


SUBLANES = 8
NET16 = tuple(_oddeven_sort(0, PEER_TOPK - 1))
NET8 = tuple(_oddeven_sort(0, SUBLANES - 1))
PEER_CAND = tuple((k1, k2) for k1 in range(PEER_TOPK) for k2 in range(PEER_TOPK)
                  if (k1 + 1) * (k2 + 1) <= PEER_TOPK)


def _ce(x, i, j):
    a, b = x[i], x[j]
    x[i] = jnp.maximum(a, b)
    x[j] = jnp.minimum(a, b)


def _bitonic_sort16(m):
    d = PEER_TOPK // 2
    while d >= 1:
        for i in range(PEER_TOPK):
            if (i & d) == 0:
                _ce(m, i, i + d)
        d //= 2


def _merge_sublanes(v, shifts):
    for sh in shifts:
        w = [pltpu.roll(t, sh, 0) for t in v]
        v = [jnp.maximum(v[k], w[PEER_TOPK - 1 - k]) for k in range(PEER_TOPK)]
        _bitonic_sort16(v)
    return v


def _top16_of_rows(s):
    v = [s[SUBLANES * k:SUBLANES * (k + 1), :] for k in range(PEER_N_KEYS // SUBLANES)]
    for i, j in NET16:
        _ce(v, i, j)
    return _merge_sublanes(v, (4, 2, 1))


def _peer_prep_kernel(x_ref, g_ref, wqt_ref, sk_ref, ht_ref, r2_ref, e2_ref, c_ref, cf_ref, hb_scr):
    tb = x_ref.shape[0]
    h2 = _rms(x_ref[...], g_ref[...])
    hb_scr[...] = h2.T.astype(BF16)
    ht_ref[...] = hb_scr[...]
    sub = lax.broadcasted_iota(jnp.int32, (SUBLANES, LANES), 0)
    neg = jnp.full((SUBLANES, LANES), -jnp.inf, F32)

    def head(h, carry):
        hb = hb_scr[...]
        q1 = jnp.dot(wqt_ref[pl.ds(pl.multiple_of(h * 2 * LANES, LANES), LANES), :], hb,
                     preferred_element_type=F32)
        q2 = jnp.dot(wqt_ref[pl.ds(pl.multiple_of(h * 2 * LANES + LANES, LANES), LANES), :], hb,
                     preferred_element_type=F32)
        s1f = jnp.dot(sk_ref[2 * h], q1.astype(BF16), preferred_element_type=F32)
        s2f = jnp.dot(sk_ref[2 * h + 1], q2.astype(BF16), preferred_element_type=F32)
        for lt in range(tb // LANES):
            ls = slice(lt * LANES, (lt + 1) * LANES)
            s1 = s1f[:, ls]
            s2 = s2f[:, ls]
            a = _top16_of_rows(s1)
            b = _top16_of_rows(s2)
            packed = []
            for v in range((len(PEER_CAND) + SUBLANES - 1) // SUBLANES):
                p = neg
                for s in range(SUBLANES):
                    c = SUBLANES * v + s
                    if c < len(PEER_CAND):
                        k1, k2 = PEER_CAND[c]
                        p = jnp.where(sub == s, a[k1] + b[k2], p)
                packed.append(p)
            while len(packed) < SUBLANES:
                packed.append(neg)
            for i, j in NET8:
                _ce(packed, i, j)
            w = [pltpu.roll(t, 4, 0) for t in packed]
            m = packed + w[::-1]
            _bitonic_sort16(m)
            m = _merge_sublanes(m, (2, 1))
            thr = m[PEER_TOPK - 1][0:1, :]
            zsum = m[0] - m[0] + 1.0
            for k in range(1, PEER_TOPK):
                zsum = zsum + jnp.exp(m[k] - m[0])
            zinv = 1.0 / zsum[0:1, :]
            cnt = jnp.zeros((PEER_N_KEYS, LANES), F32)
            rank = jnp.zeros((PEER_N_KEYS, LANES), F32)
            for k in range(PEER_TOPK):
                bk = b[k][0:1, :]
                cnt = cnt + jnp.where(s1 + bk >= thr, 1.0, 0.0)
                rank = rank + jnp.where(bk > s2, 1.0, 0.0)
            r2_ref[h, :, ls] = rank.astype(r2_ref.dtype)
            e2_ref[h, :, ls] = jnp.exp(s2 - b[0][0:1, :]).astype(e2_ref.dtype)
            c_ref[h, :, ls] = cnt
            cf_ref[h, :, ls] = jnp.exp(s1 - a[0][0:1, :]) * zinv
        return carry

    lax.fori_loop(0, PEER_HEADS, head, 0)


def _peer_prep(x, gain, w_qt, sub_keys, tb=256):
    T = x.shape[0]
    tb = min(tb, T)
    aux_spec = pl.BlockSpec((PEER_HEADS, PEER_N_KEYS, tb), lambda i: (0, 0, i))
    aux_f32 = jax.ShapeDtypeStruct((PEER_HEADS, PEER_N_KEYS, T), F32)
    aux_b16 = jax.ShapeDtypeStruct((PEER_HEADS, PEER_N_KEYS, T), BF16)
    return pl.pallas_call(
        _peer_prep_kernel,
        grid=(T // tb,),
        in_specs=[pl.BlockSpec((tb, D_MODEL), lambda i: (i, 0)),
                  pl.BlockSpec((1, D_MODEL), lambda i: (0, 0)),
                  pl.BlockSpec((2 * PEER_HEADS * LANES, D_MODEL), lambda i: (0, 0)),
                  pl.BlockSpec((2 * PEER_HEADS, PEER_N_KEYS, LANES), lambda i: (0, 0, 0))],
        out_specs=[pl.BlockSpec((D_MODEL, tb), lambda i: (0, i)),
                   aux_spec, aux_spec, aux_spec, aux_spec],
        out_shape=[jax.ShapeDtypeStruct((D_MODEL, T), BF16),
                   aux_b16, aux_b16, aux_f32, aux_f32],
        scratch_shapes=[pltpu.VMEM((D_MODEL, tb), BF16)],
        compiler_params=_cparams(("parallel",), VMEM_LIMIT),
        name="peer_prep",
    )(x, gain, w_qt, sub_keys)


PEER_ET = 2048
PEER_MM = 256


GELU_K1 = -2.0 * float(np.sqrt(2.0 / np.pi)) * float(np.log2(np.e))
GELU_K3 = GELU_K1 * 0.044715


def _peer_dense_kernel(x_ref, ht_ref, r2_ref, e2_ref, c_ref, cf_ref, u_ref, vt_ref, o_ref,
                       acc_scr, a_scr, ga_scr):
    et = pl.program_id(1)
    n_sub = PEER_ET // PEER_N_KEYS

    @pl.when(et == 0)
    def _():
        acc_scr[...] = jnp.zeros_like(acc_scr)

    hb = ht_ref[...]
    n_pair = PEER_ET // PEER_MM
    per_mm = PEER_MM // PEER_N_KEYS
    for pr in range(n_pair):
        ps = slice(pr * PEER_MM, (pr + 1) * PEER_MM)
        a_scr[ps, :] = jnp.dot(u_ref[ps, :], hb, preferred_element_type=F32)
    for pr in range(n_pair):
        ps = slice(pr * PEER_MM, (pr + 1) * PEER_MM)
        for half in range(per_mm):
            rs = slice(pr * PEER_MM + half * PEER_N_KEYS, pr * PEER_MM + (half + 1) * PEER_N_KEYS)
            i = et * n_sub + pr * per_mm + half
            a = a_scr[rs, :].astype(BF16)
            act = a / (1.0 + jnp.exp2(a * (GELU_K1 + GELU_K3 * (a * a))))
            g = None
            for h in range(PEER_HEADS):
                c_row = c_ref[h, pl.ds(i, 1), :].astype(r2_ref.dtype)
                cf_row = cf_ref[h, pl.ds(i, 1), :].astype(e2_ref.dtype)
                t = jnp.where(r2_ref[h] < c_row, e2_ref[h] * cf_row, jnp.zeros((), e2_ref.dtype))
                g = t if g is None else g + t
            ga_scr[rs, :] = g.astype(BF16) * act
    acc_scr[...] += jnp.dot(vt_ref[...], ga_scr[...], preferred_element_type=F32)

    @pl.when(et == pl.num_programs(1) - 1)
    def _():
        o_ref[...] = x_ref[...] + acc_scr[...].T


def _peer_dense(x, ht, r2, e2, cnt, coef, u_tab, vt_tab, tb=512):
    T = x.shape[0]
    tb = min(tb, T)
    n_exp = u_tab.shape[0]
    aux_spec = pl.BlockSpec((PEER_HEADS, PEER_N_KEYS, tb), lambda i, e: (0, 0, i))
    return pl.pallas_call(
        _peer_dense_kernel,
        grid=(T // tb, n_exp // PEER_ET),
        in_specs=[pl.BlockSpec((tb, D_MODEL), lambda i, e: (i, 0)),
                  pl.BlockSpec((D_MODEL, tb), lambda i, e: (0, i)),
                  aux_spec, aux_spec, aux_spec, aux_spec,
                  pl.BlockSpec((PEER_ET, D_MODEL), lambda i, e: (e, 0)),
                  pl.BlockSpec((D_MODEL, PEER_ET), lambda i, e: (0, e))],
        out_specs=pl.BlockSpec((tb, D_MODEL), lambda i, e: (i, 0)),
        out_shape=jax.ShapeDtypeStruct((T, D_MODEL), F32),
        scratch_shapes=[pltpu.VMEM((D_MODEL, tb), F32),
                        pltpu.VMEM((PEER_ET, tb), F32),
                        pltpu.VMEM((PEER_ET, tb), BF16)],
        compiler_params=_cparams(("parallel", "arbitrary"), VMEM_LIMIT),
        name="peer_dense",
    )(x, ht, r2, e2, cnt, coef, u_tab, vt_tab)


def _final_kernel(x_ref, g_ref, o_ref):
    o_ref[...] = _rms(x_ref[...], g_ref[...])


def _final_norm(x, gain, tb=1024):
    T = x.shape[0]
    tb = min(tb, T)
    return pl.pallas_call(
        _final_kernel,
        grid=(T // tb,),
        in_specs=[pl.BlockSpec((tb, D_MODEL), lambda i: (i, 0)),
                  pl.BlockSpec((1, D_MODEL), lambda i: (0, 0))],
        out_specs=pl.BlockSpec((tb, D_MODEL), lambda i: (i, 0)),
        out_shape=jax.ShapeDtypeStruct((T, D_MODEL), F32),
        compiler_params=_cparams(("parallel",)),
        name="final_norm",
    )(x, gain)


def _rope_tables(positions):
    pos = positions.astype(F32).reshape(-1, 1)

    def tables(dim):
        half = dim // 2
        inv = ROPE_THETA ** (-jnp.arange(half, dtype=F32) / half)
        ang = pos * inv
        cos = jnp.cos(ang)
        sin = jnp.sin(ang)
        reps = LANES // dim
        return (jnp.tile(jnp.concatenate([cos, cos], axis=-1), (1, reps)),
                jnp.tile(jnp.concatenate([-sin, sin], axis=-1), (1, reps)))

    cos_c, sin_c = tables(C_HEAD_DIM)
    cos_i, sin_i = tables(IDX_DIM)
    return cos_c, sin_c, cos_i, sin_i


def _pack_w_in(w):
    n_idx = IDX_DIM + IDX_HEADS
    main = w[:, :N_MAIN]
    idx = w[:, N_MAIN:N_MAIN + n_idx]
    gates = w[:, N_MAIN + n_idx:]
    pad = jnp.zeros((w.shape[0], LANES - n_idx), w.dtype)
    return jnp.concatenate([gates, main, idx, pad], axis=1).astype(BF16)


def _layer(x, l, B, S, tabs, tri, lbp_all, p):
    proj = _inproj(x, p["norm1_gain"][l][None, :], _pack_w_in(p["w_in"][l]))
    y_a = _hgrn(proj, lbp_all[l], p["hgrn_norm_gain"][l][None, :], B, S)
    bias_full = jnp.repeat(p["gmlp_b_s"][l].T, B_BLOCK, axis=1)
    y_b = _gmlp(proj, p["gmlp_norm_gain"][l][None, :], p["gmlp_w_s"][l], bias_full)
    q_r, k_r, v_t, qi_r, ki_r = _dsa_prep(proj, *tabs)
    y_c = _dsa(q_r, qi_r, proj, k_r, v_t, ki_r, tri, B, S)
    x = _merge(y_a, y_b, y_c, proj, p["w_branch"][l].astype(BF16), p["w_out"][l].astype(BF16), x)
    sk = p["peer_sub_keys"][l].reshape(2 * PEER_HEADS, PEER_N_KEYS, LANES).astype(BF16)
    ht, r2, e2, cnt, coef = _peer_prep(x, p["norm2_gain"][l][None, :],
                                       p["peer_w_q"][l].T.astype(BF16), sk)
    return _peer_dense(x, ht, r2, e2, cnt, coef, p["peer_u"][l].astype(BF16),
                       p["peer_v"][l].T.astype(BF16))


def kernel(x, positions, norm1_gain, w_in, hgrn_lb_logits, hgrn_norm_gain, gmlp_norm_gain,
           gmlp_w_s, gmlp_b_s, w_branch, w_out, norm2_gain, peer_w_q, peer_sub_keys,
           peer_u, peer_v, final_gain):
    B, S, _ = x.shape
    depth = w_in.shape[0]
    p = dict(norm1_gain=norm1_gain, w_in=w_in, hgrn_norm_gain=hgrn_norm_gain,
             gmlp_norm_gain=gmlp_norm_gain, gmlp_w_s=gmlp_w_s, gmlp_b_s=gmlp_b_s,
             w_branch=w_branch, w_out=w_out, norm2_gain=norm2_gain, peer_w_q=peer_w_q,
             peer_sub_keys=peer_sub_keys, peer_u=peer_u, peer_v=peer_v)
    sm = jax.nn.softmax(hgrn_lb_logits.astype(F32), axis=0)
    cs = jnp.cumsum(sm, axis=0)
    lb = cs - cs[0:1]
    lbp_all = jnp.stack([jnp.log(lb), jnp.log1p(-lb), 1.0 - lb], axis=1)
    tabs = _rope_tables(positions)
    r = lax.broadcasted_iota(jnp.int32, (DSA_TK, DSA_TK), 0)
    c = lax.broadcasted_iota(jnp.int32, (DSA_TK, DSA_TK), 1)
    tri = (r >= c).astype(BF16)
    xt = x.reshape(B * S, D_MODEL)
    for l in range(depth):
        xt = _layer(xt, l, B, S, tabs, tri, lbp_all, p)
    return _final_norm(xt, final_gain[None, :]).reshape(B, S, D_MODEL)
```

```python
import functools

import numpy as np
import jax
import jax.numpy as jnp
from jax import lax
from jax.experimental import pallas as pl
from jax.experimental.pallas import tpu as pltpu

F32 = jnp.float32
BF16 = jnp.bfloat16

D_MODEL = 1024
CHUNK = 64
EPS = 1e-6
ROPE_THETA = 10000.0
A_HEADS = 4
A_DK = 128
B_GROUPS = 4
B_BLOCK = 128
C_HEADS = 4
C_HEAD_DIM = 128
IDX_HEADS = 8
IDX_DIM = 64
DSA_TOPK_MAX = 256
Q_BLOCK = 128
N_BRANCH = 3
BRANCH_WIDTH = 512
PEER_HEADS = 8
PEER_N_KEYS = 128
PEER_TOPK = 16

LANES = 128
VMEM_LIMIT = 56 * 1024 * 1024

N_MAIN = 10 * 512
N_PACKED = N_BRANCH * D_MODEL + N_MAIN + LANES
COLBLK_MAIN = (N_BRANCH * D_MODEL) // 512
COLBLK_IDX = (N_BRANCH * D_MODEL + N_MAIN) // LANES

NEG_BIG = -1e30
INT_MIN = -2147483648


def _cparams(sem, vmem=None):
    return pltpu.CompilerParams(dimension_semantics=sem, vmem_limit_bytes=vmem)


def _rms(x, gain):
    return x * lax.rsqrt(jnp.mean(x * x, axis=-1, keepdims=True) + EPS) * gain


def _dot_t(a, b):
    return lax.dot_general(a, b, (((1,), (1,)), ((), ())), preferred_element_type=F32)


def _inproj_kernel(x_ref, g_ref, w_ref, o_ref, h_scr):
    @pl.when(pl.program_id(1) == 0)
    def _():
        h_scr[...] = _rms(x_ref[...], g_ref[...]).astype(BF16)

    o_ref[...] = jnp.dot(h_scr[...], w_ref[...], preferred_element_type=F32)


def _inproj(x, gain, w_packed, tm=1024, tn=1664):
    T = x.shape[0]
    tm = min(tm, T)
    return pl.pallas_call(
        _inproj_kernel,
        grid=(T // tm, N_PACKED // tn),
        in_specs=[pl.BlockSpec((tm, D_MODEL), lambda i, j: (i, 0)),
                  pl.BlockSpec((1, D_MODEL), lambda i, j: (0, 0)),
                  pl.BlockSpec((D_MODEL, tn), lambda i, j: (0, j))],
        out_specs=pl.BlockSpec((tm, tn), lambda i, j: (i, j)),
        out_shape=jax.ShapeDtypeStruct((T, N_PACKED), F32),
        scratch_shapes=[pltpu.VMEM((tm, D_MODEL), BF16)],
        compiler_params=_cparams(("parallel", "arbitrary"), VMEM_LIMIT),
        name="inproj",
    )(x, gain, w_packed)


HG_C = 128
HG_SUB = 8


def _hgrn_kernel(q_ref, f_ref, i_ref, g_ref, lbp_ref, gain_ref, o_ref, st_scr):
    @pl.when(pl.program_id(1) == 0)
    def _():
        st_scr[...] = jnp.zeros_like(st_scr)

    C = HG_C
    row = lax.broadcasted_iota(jnp.int32, (C, 1), 0)
    rowm = lax.broadcasted_iota(jnp.int32, (C, C), 0)
    colm = lax.broadcasted_iota(jnp.int32, (C, C), 1)
    gain = gain_ref[...]

    for h in range(A_HEADS):
        sl = slice(h * A_DK, (h + 1) * A_DK)
        z = f_ref[:, sl]
        loglb = lbp_ref[0:1, sl]
        log1mlb = lbp_ref[1:2, sl]
        omlb = lbp_ref[2:3, sl]
        lsig = -(jnp.maximum(-z, 0.0) + jnp.log1p(jnp.exp(-jnp.abs(z))))
        cterm = log1mlb + lsig
        logf = jnp.maximum(loglb, cterm) + jnp.log1p(jnp.exp(-jnp.abs(loglb - cterm)))
        key = omlb * jax.nn.sigmoid(-z)
        qf = jax.nn.silu(q_ref[:, sl]) * (A_DK ** -0.5)
        vf = i_ref[:, sl]

        b = logf
        sh = 1
        while sh < C:
            b = b + jnp.where(row >= sh, pltpu.roll(b, sh, 0), 0.0)
            sh *= 2

        attn = jnp.zeros((C, C), F32)
        m = C // 2
        while m >= HG_SUB:
            ref = jnp.concatenate(
                [jnp.broadcast_to(b[j * 2 * m + m - 1:j * 2 * m + m, :], (2 * m, A_DK))
                 for j in range(C // (2 * m))], axis=0)
            qt = qf * jnp.exp(jnp.minimum(b - ref, 0.0))
            kt = key * jnp.exp(jnp.minimum(ref - b, 0.0))
            a = _dot_t(qt.astype(BF16), kt.astype(BF16))
            msk = ((rowm // (2 * m)) == (colm // (2 * m))) & ((rowm % (2 * m)) >= m) & ((colm % (2 * m)) < m)
            attn = attn + jnp.where(msk, a, 0.0)
            m //= 2

        nsub = C // HG_SUB
        for j in range(HG_SUB):
            kj = jnp.concatenate(
                [jnp.broadcast_to(key[i * HG_SUB + j:i * HG_SUB + j + 1, :], (HG_SUB, A_DK))
                 for i in range(nsub)], axis=0)
            bj = jnp.concatenate(
                [jnp.broadcast_to(b[i * HG_SUB + j:i * HG_SUB + j + 1, :], (HG_SUB, A_DK))
                 for i in range(nsub)], axis=0)
            d = jnp.exp(jnp.minimum(b - bj, 0.0)) * qf * kj
            red = jnp.sum(d, axis=-1, keepdims=True)
            hit = (colm == (rowm // HG_SUB) * HG_SUB + j) & ((rowm % HG_SUB) >= j)
            attn = jnp.where(hit, red, attn)

        st = st_scr[h]
        o = jnp.dot(attn.astype(BF16), vf.astype(BF16), preferred_element_type=F32)
        o = o + _dot_t((qf * jnp.exp(b)).astype(BF16), st.astype(BF16))
        blast = b[C - 1:C, :]
        kdec = key * jnp.exp(blast - b)
        upd = lax.dot_general(vf.astype(BF16), kdec.astype(BF16), (((0,), (0,)), ((), ())),
                              preferred_element_type=F32)
        st_scr[h] = st * jnp.exp(blast) + upd

        o = _rms(o, gain)
        o_ref[:, sl] = o * jax.nn.silu(g_ref[:, sl])


def _hgrn(proj, lbp, gain, B, S):
    nc = S // HG_C
    cb = COLBLK_MAIN

    def spec(k):
        return pl.BlockSpec((HG_C, 512), lambda b, c, k=k: (b * nc + c, cb + k))

    return pl.pallas_call(
        _hgrn_kernel,
        grid=(B, nc),
        in_specs=[spec(0), spec(1), spec(2), spec(3),
                  pl.BlockSpec((3, 512), lambda b, c: (0, 0)),
                  pl.BlockSpec((1, A_DK), lambda b, c: (0, 0))],
        out_specs=pl.BlockSpec((HG_C, 512), lambda b, c: (b * nc + c, 0)),
        out_shape=jax.ShapeDtypeStruct((B * S, 512), F32),
        scratch_shapes=[pltpu.VMEM((A_HEADS, A_DK, A_DK), F32)],
        compiler_params=_cparams(("parallel", "arbitrary")),
        name="hgrn",
    )(proj, proj, proj, proj, lbp, gain)


def _gmlp_kernel(u_ref, v_ref, g_ref, w_ref, b_ref, o_ref, *, nsub):
    r = lax.broadcasted_iota(jnp.int32, (B_BLOCK, B_BLOCK), 0)
    c = lax.broadcasted_iota(jnp.int32, (B_BLOCK, B_BLOCK), 1)
    tril = r >= c
    u = jax.nn.gelu(u_ref[...])
    v = _rms(jax.nn.gelu(v_ref[...]), g_ref[...]).astype(BF16)
    bias = b_ref[...]
    for g in range(B_GROUPS):
        w = jnp.where(tril, w_ref[g], 0.0).astype(BF16)
        cs = slice(g * LANES, (g + 1) * LANES)
        for s in range(nsub):
            rs = slice(s * B_BLOCK, (s + 1) * B_BLOCK)
            mixed = jnp.dot(w, v[rs, cs], preferred_element_type=F32) + bias[:, cs]
            o_ref[rs, cs] = u[rs, cs] * mixed


def _gmlp(proj, gain, w_s, bias_full, tb=512):
    T = proj.shape[0]
    tb = min(tb, T)
    cb = COLBLK_MAIN
    return pl.pallas_call(
        functools.partial(_gmlp_kernel, nsub=tb // B_BLOCK),
        grid=(T // tb,),
        in_specs=[pl.BlockSpec((tb, 512), lambda i: (i, cb + 4)),
                  pl.BlockSpec((tb, 512), lambda i: (i, cb + 5)),
                  pl.BlockSpec((1, 512), lambda i: (0, 0)),
                  pl.BlockSpec((B_GROUPS, B_BLOCK, B_BLOCK), lambda i: (0, 0, 0)),
                  pl.BlockSpec((B_BLOCK, 512), lambda i: (0, 0))],
        out_specs=pl.BlockSpec((tb, 512), lambda i: (i, 0)),
        out_shape=jax.ShapeDtypeStruct((T, 512), F32),
        compiler_params=_cparams(("parallel",)),
        name="gmlp",
    )(proj, proj, gain, w_s, bias_full)


def _dsa_prep_kernel(q_ref, k_ref, v_ref, iq_ref, idx_ref, cc_ref, sc_ref, ci_ref, si_ref,
                     qo_ref, ko_ref, vo_ref, qio_ref, kio_ref):
    cc = cc_ref[...]
    sc = sc_ref[...]
    ci = ci_ref[...]
    si = si_ref[...]
    lane = lax.broadcasted_iota(jnp.int32, ci.shape, 1)
    first_half = (lane % IDX_DIM) < (IDX_DIM // 2)

    def rope_i(x):
        partner = jnp.where(first_half, pltpu.roll(x, LANES - IDX_DIM // 2, 1),
                            pltpu.roll(x, IDX_DIM // 2, 1))
        return x * ci + partner * si

    for h in range(C_HEADS):
        sl = slice(h * LANES, (h + 1) * LANES)
        x = q_ref[:, sl]
        qo_ref[:, sl] = ((x * cc + pltpu.roll(x, C_HEAD_DIM // 2, 1) * sc)
                         * (C_HEAD_DIM ** -0.5)).astype(BF16)
        x = k_ref[:, sl]
        ko_ref[:, sl] = (x * cc + pltpu.roll(x, C_HEAD_DIM // 2, 1) * sc).astype(BF16)
    vo_ref[0] = v_ref[...].T.astype(BF16)
    for j in range(IDX_HEADS // 2):
        r = rope_i(iq_ref[:, j * LANES:(j + 1) * LANES])
        qio_ref[:, (2 * j) * LANES:(2 * j + 1) * LANES] = r.astype(BF16)
        qio_ref[:, (2 * j + 1) * LANES:(2 * j + 2) * LANES] = pltpu.roll(r, IDX_DIM, 1).astype(BF16)
    rk = rope_i(idx_ref[...])
    kio_ref[...] = jnp.where(lane < IDX_DIM, rk, 0.0).astype(BF16)


def _dsa_prep(proj, cos_c, sin_c, cos_i, sin_i):
    T = proj.shape[0]
    tb = DSA_TK
    cb = COLBLK_MAIN

    def pspec(k):
        return pl.BlockSpec((tb, 512), lambda i, k=k: (i, cb + k))

    tspec = pl.BlockSpec((tb, LANES), lambda i: (i, 0))
    return pl.pallas_call(
        _dsa_prep_kernel,
        grid=(T // tb,),
        in_specs=[pspec(6), pspec(7), pspec(8), pspec(9),
                  pl.BlockSpec((tb, LANES), lambda i: (i, COLBLK_IDX)),
                  tspec, tspec, tspec, tspec],
        out_specs=[pl.BlockSpec((tb, 512), lambda i: (i, 0)),
                   pl.BlockSpec((tb, 512), lambda i: (i, 0)),
                   pl.BlockSpec((1, 512, tb), lambda i: (i, 0, 0)),
                   pl.BlockSpec((tb, IDX_HEADS * LANES), lambda i: (i, 0)),
                   pl.BlockSpec((tb, LANES), lambda i: (i, 0))],
        out_shape=[jax.ShapeDtypeStruct((T, 512), BF16),
                   jax.ShapeDtypeStruct((T, 512), BF16),
                   jax.ShapeDtypeStruct((T // tb, 512, tb), BF16),
                   jax.ShapeDtypeStruct((T, IDX_HEADS * LANES), BF16),
                   jax.ShapeDtypeStruct((T, LANES), BF16)],
        compiler_params=_cparams(("parallel",)),
        name="dsa_prep",
    )(proj, proj, proj, proj, proj, cos_c, sin_c, cos_i, sin_i)


DSA_TK = 1024


def _sortable(score):
    score = jnp.where(score == 0.0, 0.0, score)
    bits = pltpu.bitcast(score, jnp.int32)
    return bits ^ ((bits >> 31) & 0x7FFFFFFF)


DSA_NACC = 32
I16_MIN = -32768


def _dsa_kernel(q_ref, qi_ref, idx_ref, k_ref, vt_ref, ki_ref, trit_ref, o_ref,
                qidx_scr, key_scr, hi_scr, lo_scr, acc_scr, *, topk):
    TK = DSA_TK
    I16 = jnp.int16
    qb = pl.program_id(1)
    nkb = (qb * Q_BLOCK + Q_BLOCK + TK - 1) // TK
    wscale = (IDX_HEADS ** -0.5) * (IDX_DIM ** -0.5)

    for h in range(IDX_HEADS):
        qidx_scr[h * Q_BLOCK:(h + 1) * Q_BLOCK, :] = qi_ref[:, h * LANES:(h + 1) * LANES]
    iwt = idx_ref[...].T
    w_rows = [iwt[IDX_DIM + h:IDX_DIM + h + 1, :] * wscale for h in range(IDX_HEADS)]
    qchunk = (qb * Q_BLOCK + lax.broadcasted_iota(jnp.int32, (1, Q_BLOCK), 1)) // CHUNK
    row_tk = lax.broadcasted_iota(jnp.int32, (TK, 1), 0)

    def p1(kb, carry):
        off = pl.multiple_of(kb * TK, TK)
        logits = _dot_t(ki_ref[pl.ds(off, TK), :], qidx_scr[...])
        s = None
        for h in range(IDX_HEADS):
            t = jnp.maximum(logits[:, h * Q_BLOCK:(h + 1) * Q_BLOCK], 0.0) * w_rows[h]
            s = t if s is None else s + t
        adm = ((off + row_tk) // CHUNK) <= qchunk
        key = _sortable(jnp.where(adm, s, -jnp.inf))
        key_scr[kb] = key
        hi_scr[kb] = (key >> 16).astype(I16)
        lo_scr[kb] = ((key & 0xFFFF) + I16_MIN).astype(I16)
        return carry

    lax.fori_loop(0, nkb, p1, 0)

    def count16(ref, cand):
        c16 = cand.astype(I16)

        def body(kb, acc):
            t = ref[kb]
            for r in range(TK // DSA_NACC):
                blk = t[r * DSA_NACC:(r + 1) * DSA_NACC]
                acc = acc + jnp.where(blk >= c16, jnp.ones((), I16), jnp.zeros((), I16))
            return acc

        acc = lax.fori_loop(0, nkb, body, jnp.zeros((DSA_NACC, LANES), I16))
        return jnp.sum(acc.astype(jnp.int32), axis=0, keepdims=True)

    def bisect16(ref, kneed):
        zero = jnp.zeros((1, Q_BLOCK), jnp.int32)
        c0 = count16(ref, zero)
        ok = c0 >= kneed
        init = (jnp.where(ok, zero, zero + I16_MIN), jnp.where(ok, c0, zero + (1 << 20)),
                jnp.where(ok, zero, c0))

        def step(i, carry):
            base, c_acc, c_rej = carry
            cand = base | (jnp.int32(1) << (14 - i))
            c = count16(ref, cand)
            ok = c >= kneed
            return jnp.where(ok, cand, base), jnp.where(ok, c, c_acc), jnp.where(ok, c_rej, c)

        return lax.fori_loop(0, 15, step, init)

    tau_hi, _, cnt_gt_hi = bisect16(hi_scr, topk)
    th16 = tau_hi.astype(I16)

    def keep_group(kb, carry):
        lo_scr[kb] = jnp.where(hi_scr[kb] == th16, lo_scr[kb], jnp.full((), I16_MIN, I16))
        return carry

    lax.fori_loop(0, nkb, keep_group, 0)
    tau_lo, cnt_ge_lo, cnt_gt_lo = bisect16(lo_scr, topk - cnt_gt_hi)
    cnt_gt = cnt_gt_hi + cnt_gt_lo
    tau = (tau_hi << 16) | (tau_lo - I16_MIN)
    need = (topk - cnt_gt).astype(F32)
    key_neg_inf = _sortable(jnp.full((1, 1), -jnp.inf, F32))
    tau_eff = jnp.maximum(tau, key_neg_inf + 1)

    acc_scr[...] = jnp.zeros_like(acc_scr)
    hsl = [slice(h * LANES, (h + 1) * LANES) for h in range(C_HEADS)]

    def p2(kb, carry, tie_order):
        eq_seen, ms, ls = carry
        off = pl.multiple_of(kb * TK, TK)
        kt = key_scr[kb]
        if tie_order:
            eq = kt == tau
            prefix = jnp.dot(trit_ref[...], jnp.where(eq, 1.0, 0.0).astype(BF16),
                             preferred_element_type=F32)
        ss = [_dot_t(k_ref[pl.ds(off, TK), sl], q_ref[:, sl]) for sl in hsl]
        if tie_order:
            over = jnp.where(eq_seen + prefix > need, 1, 0)
            kt = jnp.where(eq, kt - over, kt)
            eq_seen = eq_seen + prefix[TK - 1:TK, :]
        bias = jnp.where(kt >= tau_eff, 0.0, NEG_BIG)
        new_ms, new_ls, alphas, ps = [], [], [], []
        for h in range(C_HEADS):
            s = ss[h] + bias
            m_new = jnp.maximum(ms[h], jnp.max(s, axis=0, keepdims=True))
            alpha = jnp.exp(ms[h] - m_new)
            p = jnp.exp(s - m_new)
            new_ls.append(alpha * ls[h] + jnp.sum(p, axis=0, keepdims=True))
            new_ms.append(m_new)
            alphas.append(alpha)
            ps.append(p.astype(BF16))
        for h in range(C_HEADS):
            acc_scr[h] = alphas[h] * acc_scr[h] + jnp.dot(vt_ref[kb, hsl[h], :], ps[h],
                                                          preferred_element_type=F32)
        return eq_seen, tuple(new_ms), tuple(new_ls)

    row0 = jnp.zeros((1, Q_BLOCK), F32)
    init = (row0,
            tuple(row0 + NEG_BIG for _ in range(C_HEADS)),
            tuple(row0 for _ in range(C_HEADS)))
    cnt_ge = cnt_gt_hi + cnt_ge_lo
    _, ms, ls = lax.cond(
        jnp.max(cnt_ge) > topk,
        lambda: lax.fori_loop(0, nkb, functools.partial(p2, tie_order=True), init),
        lambda: lax.fori_loop(0, nkb, functools.partial(p2, tie_order=False), init))
    for h in range(C_HEADS):
        o_ref[:, h * LANES:(h + 1) * LANES] = (acc_scr[h] / ls[h]).T


def _dsa(q_r, qi_r, proj, k_r, v_t, ki_r, trit, B, S):
    nq = S // Q_BLOCK
    topk = min(DSA_TOPK_MAX, S // 4)
    nkt = S // DSA_TK
    assert nkt * DSA_TK == S, "sequence length must be a multiple of the key tile"
    return pl.pallas_call(
        functools.partial(_dsa_kernel, topk=topk),
        grid=(B, nq),
        in_specs=[pl.BlockSpec((Q_BLOCK, 512), lambda b, q: (b * nq + q, 0)),
                  pl.BlockSpec((Q_BLOCK, IDX_HEADS * LANES), lambda b, q: (b * nq + q, 0)),
                  pl.BlockSpec((Q_BLOCK, LANES), lambda b, q: (b * nq + q, COLBLK_IDX)),
                  pl.BlockSpec((S, 512), lambda b, q: (b, 0)),
                  pl.BlockSpec((nkt, 512, DSA_TK), lambda b, q: (b, 0, 0)),
                  pl.BlockSpec((S, LANES), lambda b, q: (b, 0)),
                  pl.BlockSpec((DSA_TK, DSA_TK), lambda b, q: (0, 0))],
        out_specs=pl.BlockSpec((Q_BLOCK, 512), lambda b, q: (b * nq + q, 0)),
        out_shape=jax.ShapeDtypeStruct((B * S, 512), F32),
        scratch_shapes=[pltpu.VMEM((IDX_HEADS * Q_BLOCK, LANES), BF16),
                        pltpu.VMEM((nkt, DSA_TK, Q_BLOCK), jnp.int32),
                        pltpu.VMEM((nkt, DSA_TK, Q_BLOCK), jnp.int16),
                        pltpu.VMEM((nkt, DSA_TK, Q_BLOCK), jnp.int16),
                        pltpu.VMEM((C_HEADS, LANES, Q_BLOCK), F32)],
        compiler_params=_cparams(("parallel", "arbitrary"), VMEM_LIMIT),
        name="dsa",
    )(q_r, qi_r, proj, k_r, v_t, ki_r, trit)


def _merge_kernel(ya_ref, yb_ref, yc_ref, g0_ref, g1_ref, g2_ref, wb_ref, wo_ref, x_ref, o_ref):
    mixed = None
    for y_ref, g_ref, n in ((ya_ref, g0_ref, 0), (yb_ref, g1_ref, 1), (yc_ref, g2_ref, 2)):
        bp = jnp.dot(y_ref[...].astype(BF16), wb_ref[n], preferred_element_type=F32)
        t = jax.nn.sigmoid(g_ref[...]) * bp
        mixed = t if mixed is None else mixed + t
    o_ref[...] = x_ref[...] + jnp.dot(mixed.astype(BF16), wo_ref[...], preferred_element_type=F32)


def _merge(y_a, y_b, y_c, proj, w_branch, w_out, x, tb=256):
    T = x.shape[0]
    tb = min(tb, T)
    yspec = pl.BlockSpec((tb, 512), lambda i: (i, 0))

    def gspec(n):
        return pl.BlockSpec((tb, D_MODEL), lambda i, n=n: (i, n))

    return pl.pallas_call(
        _merge_kernel,
        grid=(T // tb,),
        in_specs=[yspec, yspec, yspec, gspec(0), gspec(1), gspec(2),
                  pl.BlockSpec((N_BRANCH, BRANCH_WIDTH, D_MODEL), lambda i: (0, 0, 0)),
                  pl.BlockSpec((D_MODEL, D_MODEL), lambda i: (0, 0)),
                  pl.BlockSpec((tb, D_MODEL), lambda i: (i, 0))],
        out_specs=pl.BlockSpec((tb, D_MODEL), lambda i: (i, 0)),
        out_shape=jax.ShapeDtypeStruct((T, D_MODEL), F32),
        compiler_params=_cparams(("parallel",), VMEM_LIMIT),
        name="merge",
    )(y_a, y_b, y_c, proj, proj, proj, w_branch, w_out, x)


def _oddeven_merge(lo, hi, r):
    step = r * 2
    if step < hi - lo:
        yield from _oddeven_merge(lo, hi, step)
        yield from _oddeven_merge(lo + r, hi, step)
        yield from [(i, i + r) for i in range(lo + r, hi - r, step)]
    else:
        yield (lo, lo + r)


def _oddeven_sort(lo, hi):
    if hi - lo >= 1:
        mid = lo + (hi - lo) // 2
        yield from _oddeven_sort(lo, mid)
        yield from _oddeven_sort(mid + 1, hi)
        yield from _oddeven_merge(lo, hi, 1)


SUBLANES = 8
NET16 = tuple(_oddeven_sort(0, PEER_TOPK - 1))
NET8 = tuple(_oddeven_sort(0, SUBLANES - 1))
PEER_CAND = tuple((k1, k2) for k1 in range(PEER_TOPK) for k2 in range(PEER_TOPK)
                  if (k1 + 1) * (k2 + 1) <= PEER_TOPK)


def _ce(x, i, j):
    a, b = x[i], x[j]
    x[i] = jnp.maximum(a, b)
    x[j] = jnp.minimum(a, b)


def _bitonic_sort16(m):
    d = PEER_TOPK // 2
    while d >= 1:
        for i in range(PEER_TOPK):
            if (i & d) == 0:
                _ce(m, i, i + d)
        d //= 2


def _merge_sublanes(v, shifts):
    for sh in shifts:
        w = [pltpu.roll(t, sh, 0) for t in v]
        v = [jnp.maximum(v[k], w[PEER_TOPK - 1 - k]) for k in range(PEER_TOPK)]
        _bitonic_sort16(v)
    return v


def _top16_of_rows(s):
    v = [s[SUBLANES * k:SUBLANES * (k + 1), :] for k in range(PEER_N_KEYS // SUBLANES)]
    for i, j in NET16:
        _ce(v, i, j)
    return _merge_sublanes(v, (4, 2, 1))


def _peer_prep_kernel(x_ref, g_ref, wqt_ref, sk_ref, ht_ref, r2_ref, e2_ref, c_ref, cf_ref, hb_scr):
    tb = x_ref.shape[0]
    h2 = _rms(x_ref[...], g_ref[...])
    hb_scr[...] = h2.T.astype(BF16)
    ht_ref[...] = hb_scr[...]
    sub = lax.broadcasted_iota(jnp.int32, (SUBLANES, LANES), 0)
    neg = jnp.full((SUBLANES, LANES), -jnp.inf, F32)

    def head(h, carry):
        hb = hb_scr[...]
        q1 = jnp.dot(wqt_ref[pl.ds(pl.multiple_of(h * 2 * LANES, LANES), LANES), :], hb,
                     preferred_element_type=F32)
        q2 = jnp.dot(wqt_ref[pl.ds(pl.multiple_of(h * 2 * LANES + LANES, LANES), LANES), :], hb,
                     preferred_element_type=F32)
        s1f = jnp.dot(sk_ref[2 * h], q1.astype(BF16), preferred_element_type=F32)
        s2f = jnp.dot(sk_ref[2 * h + 1], q2.astype(BF16), preferred_element_type=F32)
        for lt in range(tb // LANES):
            ls = slice(lt * LANES, (lt + 1) * LANES)
            s1 = s1f[:, ls]
            s2 = s2f[:, ls]
            a = _top16_of_rows(s1)
            b = _top16_of_rows(s2)
            packed = []
            for v in range((len(PEER_CAND) + SUBLANES - 1) // SUBLANES):
                p = neg
                for s in range(SUBLANES):
                    c = SUBLANES * v + s
                    if c < len(PEER_CAND):
                        k1, k2 = PEER_CAND[c]
                        p = jnp.where(sub == s, a[k1] + b[k2], p)
                packed.append(p)
            while len(packed) < SUBLANES:
                packed.append(neg)
            for i, j in NET8:
                _ce(packed, i, j)
            w = [pltpu.roll(t, 4, 0) for t in packed]
            m = packed + w[::-1]
            _bitonic_sort16(m)
            m = _merge_sublanes(m, (2, 1))
            thr = m[PEER_TOPK - 1][0:1, :]
            zsum = m[0] - m[0] + 1.0
            for k in range(1, PEER_TOPK):
                zsum = zsum + jnp.exp(m[k] - m[0])
            zinv = 1.0 / zsum[0:1, :]
            cnt = jnp.zeros((PEER_N_KEYS, LANES), F32)
            rank = jnp.zeros((PEER_N_KEYS, LANES), F32)
            for k in range(PEER_TOPK):
                bk = b[k][0:1, :]
                cnt = cnt + jnp.where(s1 + bk >= thr, 1.0, 0.0)
                rank = rank + jnp.where(bk > s2, 1.0, 0.0)
            r2_ref[h, :, ls] = rank.astype(r2_ref.dtype)
            e2_ref[h, :, ls] = jnp.exp(s2 - b[0][0:1, :]).astype(e2_ref.dtype)
            c_ref[h, :, ls] = cnt
            cf_ref[h, :, ls] = jnp.exp(s1 - a[0][0:1, :]) * zinv
        return carry

    lax.fori_loop(0, PEER_HEADS, head, 0)


def _peer_prep(x, gain, w_qt, sub_keys, tb=256):
    T = x.shape[0]
    tb = min(tb, T)
    aux_spec = pl.BlockSpec((PEER_HEADS, PEER_N_KEYS, tb), lambda i: (0, 0, i))
    aux_f32 = jax.ShapeDtypeStruct((PEER_HEADS, PEER_N_KEYS, T), F32)
    aux_b16 = jax.ShapeDtypeStruct((PEER_HEADS, PEER_N_KEYS, T), BF16)
    return pl.pallas_call(
        _peer_prep_kernel,
        grid=(T // tb,),
        in_specs=[pl.BlockSpec((tb, D_MODEL), lambda i: (i, 0)),
                  pl.BlockSpec((1, D_MODEL), lambda i: (0, 0)),
                  pl.BlockSpec((2 * PEER_HEADS * LANES, D_MODEL), lambda i: (0, 0)),
                  pl.BlockSpec((2 * PEER_HEADS, PEER_N_KEYS, LANES), lambda i: (0, 0, 0))],
        out_specs=[pl.BlockSpec((D_MODEL, tb), lambda i: (0, i)),
                   aux_spec, aux_spec, aux_spec, aux_spec],
        out_shape=[jax.ShapeDtypeStruct((D_MODEL, T), BF16),
                   aux_b16, aux_b16, aux_f32, aux_f32],
        scratch_shapes=[pltpu.VMEM((D_MODEL, tb), BF16)],
        compiler_params=_cparams(("parallel",), VMEM_LIMIT),
        name="peer_prep",
    )(x, gain, w_qt, sub_keys)


PEER_ET = 2048
PEER_MM = 256


GELU_K1 = -2.0 * float(np.sqrt(2.0 / np.pi)) * float(np.log2(np.e))
GELU_K3 = GELU_K1 * 0.044715


def _peer_dense_kernel(x_ref, ht_ref, r2_ref, e2_ref, c_ref, cf_ref, u_ref, vt_ref, o_ref,
                       acc_scr, a_scr, ga_scr):
    et = pl.program_id(1)
    n_sub = PEER_ET // PEER_N_KEYS

    @pl.when(et == 0)
    def _():
        acc_scr[...] = jnp.zeros_like(acc_scr)

    hb = ht_ref[...]
    n_pair = PEER_ET // PEER_MM
    per_mm = PEER_MM // PEER_N_KEYS
    for pr in range(n_pair):
        ps = slice(pr * PEER_MM, (pr + 1) * PEER_MM)
        a_scr[ps, :] = jnp.dot(u_ref[ps, :], hb, preferred_element_type=F32)
    for pr in range(n_pair):
        ps = slice(pr * PEER_MM, (pr + 1) * PEER_MM)
        for half in range(per_mm):
            rs = slice(pr * PEER_MM + half * PEER_N_KEYS, pr * PEER_MM + (half + 1) * PEER_N_KEYS)
            i = et * n_sub + pr * per_mm + half
            a = a_scr[rs, :].astype(BF16)
            act = a / (1.0 + jnp.exp2(a * (GELU_K1 + GELU_K3 * (a * a))))
            g = None
            for h in range(PEER_HEADS):
                c_row = c_ref[h, pl.ds(i, 1), :].astype(r2_ref.dtype)
                cf_row = cf_ref[h, pl.ds(i, 1), :].astype(e2_ref.dtype)
                t = jnp.where(r2_ref[h] < c_row, e2_ref[h] * cf_row, jnp.zeros((), e2_ref.dtype))
                g = t if g is None else g + t
            ga_scr[rs, :] = g.astype(BF16) * act
    acc_scr[...] += jnp.dot(vt_ref[...], ga_scr[...], preferred_element_type=F32)

    @pl.when(et == pl.num_programs(1) - 1)
    def _():
        o_ref[...] = x_ref[...] + acc_scr[...].T


def _peer_dense(x, ht, r2, e2, cnt, coef, u_tab, vt_tab, tb=512):
    T = x.shape[0]
    tb = min(tb, T)
    n_exp = u_tab.shape[0]
    aux_spec = pl.BlockSpec((PEER_HEADS, PEER_N_KEYS, tb), lambda i, e: (0, 0, i))
    return pl.pallas_call(
        _peer_dense_kernel,
        grid=(T // tb, n_exp // PEER_ET),
        in_specs=[pl.BlockSpec((tb, D_MODEL), lambda i, e: (i, 0)),
                  pl.BlockSpec((D_MODEL, tb), lambda i, e: (0, i)),
                  aux_spec, aux_spec, aux_spec, aux_spec,
                  pl.BlockSpec((PEER_ET, D_MODEL), lambda i, e: (e, 0)),
                  pl.BlockSpec((D_MODEL, PEER_ET), lambda i, e: (0, e))],
        out_specs=pl.BlockSpec((tb, D_MODEL), lambda i, e: (i, 0)),
        out_shape=jax.ShapeDtypeStruct((T, D_MODEL), F32),
        scratch_shapes=[pltpu.VMEM((D_MODEL, tb), F32),
                        pltpu.VMEM((PEER_ET, tb), F32),
                        pltpu.VMEM((PEER_ET, tb), BF16)],
        compiler_params=_cparams(("parallel", "arbitrary"), VMEM_LIMIT),
        name="peer_dense",
    )(x, ht, r2, e2, cnt, coef, u_tab, vt_tab)


def _final_kernel(x_ref, g_ref, o_ref):
    o_ref[...] = _rms(x_ref[...], g_ref[...])


def _final_norm(x, gain, tb=1024):
    T = x.shape[0]
    tb = min(tb, T)
    return pl.pallas_call(
        _final_kernel,
        grid=(T // tb,),
        in_specs=[pl.BlockSpec((tb, D_MODEL), lambda i: (i, 0)),
                  pl.BlockSpec((1, D_MODEL), lambda i: (0, 0))],
        out_specs=pl.BlockSpec((tb, D_MODEL), lambda i: (i, 0)),
        out_shape=jax.ShapeDtypeStruct((T, D_MODEL), F32),
        compiler_params=_cparams(("parallel",)),
        name="final_norm",
    )(x, gain)


def _rope_tables(positions):
    pos = positions.astype(F32).reshape(-1, 1)

    def tables(dim):
        half = dim // 2
        inv = ROPE_THETA ** (-jnp.arange(half, dtype=F32) / half)
        ang = pos * inv
        cos = jnp.cos(ang)
        sin = jnp.sin(ang)
        reps = LANES // dim
        return (jnp.tile(jnp.concatenate([cos, cos], axis=-1), (1, reps)),
                jnp.tile(jnp.concatenate([-sin, sin], axis=-1), (1, reps)))

    cos_c, sin_c = tables(C_HEAD_DIM)
    cos_i, sin_i = tables(IDX_DIM)
    return cos_c, sin_c, cos_i, sin_i


def _pack_w_in(w):
    n_idx = IDX_DIM + IDX_HEADS
    main = w[:, :N_MAIN]
    idx = w[:, N_MAIN:N_MAIN + n_idx]
    gates = w[:, N_MAIN + n_idx:]
    pad = jnp.zeros((w.shape[0], LANES - n_idx), w.dtype)
    return jnp.concatenate([gates, main, idx, pad], axis=1).astype(BF16)


def _layer(x, l, B, S, tabs, tri, lbp_all, p):
    proj = _inproj(x, p["norm1_gain"][l][None, :], _pack_w_in(p["w_in"][l]))
    y_a = _hgrn(proj, lbp_all[l], p["hgrn_norm_gain"][l][None, :], B, S)
    bias_full = jnp.repeat(p["gmlp_b_s"][l].T, B_BLOCK, axis=1)
    y_b = _gmlp(proj, p["gmlp_norm_gain"][l][None, :], p["gmlp_w_s"][l], bias_full)
    q_r, k_r, v_t, qi_r, ki_r = _dsa_prep(proj, *tabs)
    y_c = _dsa(q_r, qi_r, proj, k_r, v_t, ki_r, tri, B, S)
    x = _merge(y_a, y_b, y_c, proj, p["w_branch"][l].astype(BF16), p["w_out"][l].astype(BF16), x)
    sk = p["peer_sub_keys"][l].reshape(2 * PEER_HEADS, PEER_N_KEYS, LANES).astype(BF16)
    ht, r2, e2, cnt, coef = _peer_prep(x, p["norm2_gain"][l][None, :],
                                       p["peer_w_q"][l].T.astype(BF16), sk)
    return _peer_dense(x, ht, r2, e2, cnt, coef, p["peer_u"][l].astype(BF16),
                       p["peer_v"][l].T.astype(BF16))


def kernel(x, positions, norm1_gain, w_in, hgrn_lb_logits, hgrn_norm_gain, gmlp_norm_gain,
           gmlp_w_s, gmlp_b_s, w_branch, w_out, norm2_gain, peer_w_q, peer_sub_keys,
           peer_u, peer_v, final_gain):
    B, S, _ = x.shape
    depth = w_in.shape[0]
    p = dict(norm1_gain=norm1_gain, w_in=w_in, hgrn_norm_gain=hgrn_norm_gain,
             gmlp_norm_gain=gmlp_norm_gain, gmlp_w_s=gmlp_w_s, gmlp_b_s=gmlp_b_s,
             w_branch=w_branch, w_out=w_out, norm2_gain=norm2_gain, peer_w_q=peer_w_q,
             peer_sub_keys=peer_sub_keys, peer_u=peer_u, peer_v=peer_v)
    sm = jax.nn.softmax(hgrn_lb_logits.astype(F32), axis=0)
    cs = jnp.cumsum(sm, axis=0)
    lb = cs - cs[0:1]
    lbp_all = jnp.stack([jnp.log(lb), jnp.log1p(-lb), 1.0 - lb], axis=1)
    tabs = _rope_tables(positions)
    r = lax.broadcasted_iota(jnp.int32, (DSA_TK, DSA_TK), 0)
    c = lax.broadcasted_iota(jnp.int32, (DSA_TK, DSA_TK), 1)
    tri = (r >= c).astype(BF16)
    xt = x.reshape(B * S, D_MODEL)
    for l in range(depth):
        xt = _layer(xt, l, B, S, tabs, tri, lbp_all, p)
    return _final_norm(xt, final_gain[None, :]).reshape(B, S, D_MODEL)
```

```python
import functools

import numpy as np
import jax
import jax.numpy as jnp
from jax import lax
from jax.experimental import pallas as pl
from jax.experimental.pallas import tpu as pltpu

F32 = jnp.float32
BF16 = jnp.bfloat16

D_MODEL = 1024
CHUNK = 64
EPS = 1e-6
ROPE_THETA = 10000.0
A_HEADS = 4
A_DK = 128
B_GROUPS = 4
B_BLOCK = 128
C_HEADS = 4
C_HEAD_DIM = 128
IDX_HEADS = 8
IDX_DIM = 64
DSA_TOPK_MAX = 256
Q_BLOCK = 128
N_BRANCH = 3
BRANCH_WIDTH = 512
PEER_HEADS = 8
PEER_N_KEYS = 128
PEER_TOPK = 16

LANES = 128
VMEM_LIMIT = 56 * 1024 * 1024

N_MAIN = 10 * 512
N_PACKED = N_BRANCH * D_MODEL + N_MAIN + LANES
COLBLK_MAIN = (N_BRANCH * D_MODEL) // 512
COLBLK_IDX = (N_BRANCH * D_MODEL + N_MAIN) // LANES

NEG_BIG = -1e30
INT_MIN = -2147483648


def _cparams(sem, vmem=None):
    return pltpu.CompilerParams(dimension_semantics=sem, vmem_limit_bytes=vmem)


def _rms(x, gain):
    return x * lax.rsqrt(jnp.mean(x * x, axis=-1, keepdims=True) + EPS) * gain


def _dot_t(a, b):
    return lax.dot_general(a, b, (((1,), (1,)), ((), ())), preferred_element_type=F32)


def _inproj_kernel(x_ref, g_ref, w_ref, o_ref, h_scr):
    @pl.when(pl.program_id(1) == 0)
    def _():
        h_scr[...] = _rms(x_ref[...], g_ref[...]).astype(BF16)

    o_ref[...] = jnp.dot(h_scr[...], w_ref[...], preferred_element_type=F32)


def _inproj(x, gain, w_packed, tm=1024, tn=1664):
    T = x.shape[0]
    tm = min(tm, T)
    return pl.pallas_call(
        _inproj_kernel,
        grid=(T // tm, N_PACKED // tn),
        in_specs=[pl.BlockSpec((tm, D_MODEL), lambda i, j: (i, 0)),
                  pl.BlockSpec((1, D_MODEL), lambda i, j: (0, 0)),
                  pl.BlockSpec((D_MODEL, tn), lambda i, j: (0, j))],
        out_specs=pl.BlockSpec((tm, tn), lambda i, j: (i, j)),
        out_shape=jax.ShapeDtypeStruct((T, N_PACKED), F32),
        scratch_shapes=[pltpu.VMEM((tm, D_MODEL), BF16)],
        compiler_params=_cparams(("parallel", "arbitrary"), VMEM_LIMIT),
        name="inproj",
    )(x, gain, w_packed)


HG_C = 128
HG_SUB = 8


def _hgrn_kernel(q_ref, f_ref, i_ref, g_ref, lbp_ref, gain_ref, o_ref, st_scr):
    @pl.when(pl.program_id(1) == 0)
    def _():
        st_scr[...] = jnp.zeros_like(st_scr)

    C = HG_C
    row = lax.broadcasted_iota(jnp.int32, (C, 1), 0)
    rowm = lax.broadcasted_iota(jnp.int32, (C, C), 0)
    colm = lax.broadcasted_iota(jnp.int32, (C, C), 1)
    gain = gain_ref[...]

    for h in range(A_HEADS):
        sl = slice(h * A_DK, (h + 1) * A_DK)
        z = f_ref[:, sl]
        loglb = lbp_ref[0:1, sl]
        log1mlb = lbp_ref[1:2, sl]
        omlb = lbp_ref[2:3, sl]
        lsig = -(jnp.maximum(-z, 0.0) + jnp.log1p(jnp.exp(-jnp.abs(z))))
        cterm = log1mlb + lsig
        logf = jnp.maximum(loglb, cterm) + jnp.log1p(jnp.exp(-jnp.abs(loglb - cterm)))
        key = omlb * jax.nn.sigmoid(-z)
        qf = jax.nn.silu(q_ref[:, sl]) * (A_DK ** -0.5)
        vf = i_ref[:, sl]

        b = logf
        sh = 1
        while sh < C:
            b = b + jnp.where(row >= sh, pltpu.roll(b, sh, 0), 0.0)
            sh *= 2

        attn = jnp.zeros((C, C), F32)
        m = C // 2
        while m >= HG_SUB:
            ref = jnp.concatenate(
                [jnp.broadcast_to(b[j * 2 * m + m - 1:j * 2 * m + m, :], (2 * m, A_DK))
                 for j in range(C // (2 * m))], axis=0)
            qt = qf * jnp.exp(jnp.minimum(b - ref, 0.0))
            kt = key * jnp.exp(jnp.minimum(ref - b, 0.0))
            a = _dot_t(qt.astype(BF16), kt.astype(BF16))
            msk = ((rowm // (2 * m)) == (colm // (2 * m))) & ((rowm % (2 * m)) >= m) & ((colm % (2 * m)) < m)
            attn = attn + jnp.where(msk, a, 0.0)
            m //= 2

        nsub = C // HG_SUB
        for j in range(HG_SUB):
            kj = jnp.concatenate(
                [jnp.broadcast_to(key[i * HG_SUB + j:i * HG_SUB + j + 1, :], (HG_SUB, A_DK))
                 for i in range(nsub)], axis=0)
            bj = jnp.concatenate(
                [jnp.broadcast_to(b[i * HG_SUB + j:i * HG_SUB + j + 1, :], (HG_SUB, A_DK))
                 for i in range(nsub)], axis=0)
            d = jnp.exp(jnp.minimum(b - bj, 0.0)) * qf * kj
            red = jnp.sum(d, axis=-1, keepdims=True)
            hit = (colm == (rowm // HG_SUB) * HG_SUB + j) & ((rowm % HG_SUB) >= j)
            attn = jnp.where(hit, red, attn)

        st = st_scr[h]
        o = jnp.dot(attn.astype(BF16), vf.astype(BF16), preferred_element_type=F32)
        o = o + _dot_t((qf * jnp.exp(b)).astype(BF16), st.astype(BF16))
        blast = b[C - 1:C, :]
        kdec = key * jnp.exp(blast - b)
        upd = lax.dot_general(vf.astype(BF16), kdec.astype(BF16), (((0,), (0,)), ((), ())),
                              preferred_element_type=F32)
        st_scr[h] = st * jnp.exp(blast) + upd

        o = _rms(o, gain)
        o_ref[:, sl] = o * jax.nn.silu(g_ref[:, sl])


def _hgrn(proj, lbp, gain, B, S):
    nc = S // HG_C
    cb = COLBLK_MAIN

    def spec(k):
        return pl.BlockSpec((HG_C, 512), lambda b, c, k=k: (b * nc + c, cb + k))

    return pl.pallas_call(
        _hgrn_kernel,
        grid=(B, nc),
        in_specs=[spec(0), spec(1), spec(2), spec(3),
                  pl.BlockSpec((3, 512), lambda b, c: (0, 0)),
                  pl.BlockSpec((1, A_DK), lambda b, c: (0, 0))],
        out_specs=pl.BlockSpec((HG_C, 512), lambda b, c: (b * nc + c, 0)),
        out_shape=jax.ShapeDtypeStruct((B * S, 512), F32),
        scratch_shapes=[pltpu.VMEM((A_HEADS, A_DK, A_DK), F32)],
        compiler_params=_cparams(("parallel", "arbitrary")),
        name="hgrn",
    )(proj, proj, proj, proj, lbp, gain)


def _gmlp_kernel(u_ref, v_ref, g_ref, w_ref, b_ref, o_ref, *, nsub):
    r = lax.broadcasted_iota(jnp.int32, (B_BLOCK, B_BLOCK), 0)
    c = lax.broadcasted_iota(jnp.int32, (B_BLOCK, B_BLOCK), 1)
    tril = r >= c
    u = jax.nn.gelu(u_ref[...])
    v = _rms(jax.nn.gelu(v_ref[...]), g_ref[...]).astype(BF16)
    bias = b_ref[...]
    for g in range(B_GROUPS):
        w = jnp.where(tril, w_ref[g], 0.0).astype(BF16)
        cs = slice(g * LANES, (g + 1) * LANES)
        for s in range(nsub):
            rs = slice(s * B_BLOCK, (s + 1) * B_BLOCK)
            mixed = jnp.dot(w, v[rs, cs], preferred_element_type=F32) + bias[:, cs]
            o_ref[rs, cs] = u[rs, cs] * mixed


def _gmlp(proj, gain, w_s, bias_full, tb=512):
    T = proj.shape[0]
    tb = min(tb, T)
    cb = COLBLK_MAIN
    return pl.pallas_call(
        functools.partial(_gmlp_kernel, nsub=tb // B_BLOCK),
        grid=(T // tb,),
        in_specs=[pl.BlockSpec((tb, 512), lambda i: (i, cb + 4)),
                  pl.BlockSpec((tb, 512), lambda i: (i, cb + 5)),
                  pl.BlockSpec((1, 512), lambda i: (0, 0)),
                  pl.BlockSpec((B_GROUPS, B_BLOCK, B_BLOCK), lambda i: (0, 0, 0)),
                  pl.BlockSpec((B_BLOCK, 512), lambda i: (0, 0))],
        out_specs=pl.BlockSpec((tb, 512), lambda i: (i, 0)),
        out_shape=jax.ShapeDtypeStruct((T, 512), F32),
        compiler_params=_cparams(("parallel",)),
        name="gmlp",
    )(proj, proj, gain, w_s, bias_full)


def _dsa_prep_kernel(q_ref, k_ref, v_ref, iq_ref, idx_ref, cc_ref, sc_ref, ci_ref, si_ref,
                     qo_ref, ko_ref, vo_ref, qio_ref, kio_ref):
    cc = cc_ref[...]
    sc = sc_ref[...]
    ci = ci_ref[...]
    si = si_ref[...]
    lane = lax.broadcasted_iota(jnp.int32, ci.shape, 1)
    first_half = (lane % IDX_DIM) < (IDX_DIM // 2)

    def rope_i(x):
        partner = jnp.where(first_half, pltpu.roll(x, LANES - IDX_DIM // 2, 1),
                            pltpu.roll(x, IDX_DIM // 2, 1))
        return x * ci + partner * si

    for h in range(C_HEADS):
        sl = slice(h * LANES, (h + 1) * LANES)
        x = q_ref[:, sl]
        qo_ref[:, sl] = ((x * cc + pltpu.roll(x, C_HEAD_DIM // 2, 1) * sc)
                         * (C_HEAD_DIM ** -0.5)).astype(BF16)
        x = k_ref[:, sl]
        ko_ref[:, sl] = (x * cc + pltpu.roll(x, C_HEAD_DIM // 2, 1) * sc).astype(BF16)
    vo_ref[0] = v_ref[...].T.astype(BF16)
    for j in range(IDX_HEADS // 2):
        r = rope_i(iq_ref[:, j * LANES:(j + 1) * LANES])
        qio_ref[:, (2 * j) * LANES:(2 * j + 1) * LANES] = r.astype(BF16)
        qio_ref[:, (2 * j + 1) * LANES:(2 * j + 2) * LANES] = pltpu.roll(r, IDX_DIM, 1).astype(BF16)
    rk = rope_i(idx_ref[...])
    kio_ref[...] = jnp.where(lane < IDX_DIM, rk, 0.0).astype(BF16)


def _dsa_prep(proj, cos_c, sin_c, cos_i, sin_i):
    T = proj.shape[0]
    tb = DSA_TK
    cb = COLBLK_MAIN

    def pspec(k):
        return pl.BlockSpec((tb, 512), lambda i, k=k: (i, cb + k))

    tspec = pl.BlockSpec((tb, LANES), lambda i: (i, 0))
    return pl.pallas_call(
        _dsa_prep_kernel,
        grid=(T // tb,),
        in_specs=[pspec(6), pspec(7), pspec(8), pspec(9),
                  pl.BlockSpec((tb, LANES), lambda i: (i, COLBLK_IDX)),
                  tspec, tspec, tspec, tspec],
        out_specs=[pl.BlockSpec((tb, 512), lambda i: (i, 0)),
                   pl.BlockSpec((tb, 512), lambda i: (i, 0)),
                   pl.BlockSpec((1, 512, tb), lambda i: (i, 0, 0)),
                   pl.BlockSpec((tb, IDX_HEADS * LANES), lambda i: (i, 0)),
                   pl.BlockSpec((tb, LANES), lambda i: (i, 0))],
        out_shape=[jax.ShapeDtypeStruct((T, 512), BF16),
                   jax.ShapeDtypeStruct((T, 512), BF16),
                   jax.ShapeDtypeStruct((T // tb, 512, tb), BF16),
                   jax.ShapeDtypeStruct((T, IDX_HEADS * LANES), BF16),
                   jax.ShapeDtypeStruct((T, LANES), BF16)],
        compiler_params=_cparams(("parallel",)),
        name="dsa_prep",
    )(proj, proj, proj, proj, proj, cos_c, sin_c, cos_i, sin_i)


DSA_TK = 1024


def _sortable(score):
    score = jnp.where(score == 0.0, 0.0, score)
    bits = pltpu.bitcast(score, jnp.int32)
    return bits ^ ((bits >> 31) & 0x7FFFFFFF)


DSA_NACC = 32
I16_MIN = -32768


def _dsa_kernel(q_ref, qi_ref, idx_ref, k_ref, vt_ref, ki_ref, trit_ref, o_ref,
                qidx_scr, key_scr, hi_scr, lo_scr, acc_scr, *, topk):
    TK = DSA_TK
    I16 = jnp.int16
    qb = pl.program_id(1)
    nkb = (qb * Q_BLOCK + Q_BLOCK + TK - 1) // TK
    wscale = (IDX_HEADS ** -0.5) * (IDX_DIM ** -0.5)

    for h in range(IDX_HEADS):
        qidx_scr[h * Q_BLOCK:(h + 1) * Q_BLOCK, :] = qi_ref[:, h * LANES:(h + 1) * LANES]
    iwt = idx_ref[...].T
    w_rows = [iwt[IDX_DIM + h:IDX_DIM + h + 1, :] * wscale for h in range(IDX_HEADS)]
    qchunk = (qb * Q_BLOCK + lax.broadcasted_iota(jnp.int32, (1, Q_BLOCK), 1)) // CHUNK
    row_tk = lax.broadcasted_iota(jnp.int32, (TK, 1), 0)

    def p1(kb, carry):
        off = pl.multiple_of(kb * TK, TK)
        logits = _dot_t(ki_ref[pl.ds(off, TK), :], qidx_scr[...])
        s = None
        for h in range(IDX_HEADS):
            t = jnp.maximum(logits[:, h * Q_BLOCK:(h + 1) * Q_BLOCK], 0.0) * w_rows[h]
            s = t if s is None else s + t
        adm = ((off + row_tk) // CHUNK) <= qchunk
        key = _sortable(jnp.where(adm, s, -jnp.inf))
        key_scr[kb] = key
        hi_scr[kb] = (key >> 16).astype(I16)
        lo_scr[kb] = ((key & 0xFFFF) + I16_MIN).astype(I16)
        return carry

    lax.fori_loop(0, nkb, p1, 0)

    def count16(ref, cand):
        c16 = cand.astype(I16)

        def body(kb, acc):
            t = ref[kb]
            for r in range(TK // DSA_NACC):
                blk = t[r * DSA_NACC:(r + 1) * DSA_NACC]
                acc = acc + jnp.where(blk >= c16, jnp.ones((), I16), jnp.zeros((), I16))
            return acc

        acc = lax.fori_loop(0, nkb, body, jnp.zeros((DSA_NACC, LANES), I16))
        return jnp.sum(acc.astype(jnp.int32), axis=0, keepdims=True)

    def bisect16(ref, kneed):
        zero = jnp.zeros((1, Q_BLOCK), jnp.int32)
        c0 = count16(ref, zero)
        ok = c0 >= kneed
        init = (jnp.where(ok, zero, zero + I16_MIN), jnp.where(ok, c0, zero + (1 << 20)),
                jnp.where(ok, zero, c0))

        def step(i, carry):
            base, c_acc, c_rej = carry
            cand = base | (jnp.int32(1) << (14 - i))
            c = count16(ref, cand)
            ok = c >= kneed
            return jnp.where(ok, cand, base), jnp.where(ok, c, c_acc), jnp.where(ok, c_rej, c)

        return lax.fori_loop(0, 15, step, init)

    tau_hi, _, cnt_gt_hi = bisect16(hi_scr, topk)
    th16 = tau_hi.astype(I16)

    def keep_group(kb, carry):
        lo_scr[kb] = jnp.where(hi_scr[kb] == th16, lo_scr[kb], jnp.full((), I16_MIN, I16))
        return carry

    lax.fori_loop(0, nkb, keep_group, 0)
    tau_lo, cnt_ge_lo, cnt_gt_lo = bisect16(lo_scr, topk - cnt_gt_hi)
    cnt_gt = cnt_gt_hi + cnt_gt_lo
    tau = (tau_hi << 16) | (tau_lo - I16_MIN)
    need = (topk - cnt_gt).astype(F32)
    key_neg_inf = _sortable(jnp.full((1, 1), -jnp.inf, F32))
    tau_eff = jnp.maximum(tau, key_neg_inf + 1)

    acc_scr[...] = jnp.zeros_like(acc_scr)
    hsl = [slice(h * LANES, (h + 1) * LANES) for h in range(C_HEADS)]

    def p2(kb, carry, tie_order):
        eq_seen, ms, ls = carry
        off = pl.multiple_of(kb * TK, TK)
        kt = key_scr[kb]
        if tie_order:
            eq = kt == tau
            prefix = jnp.dot(trit_ref[...], jnp.where(eq, 1.0, 0.0).astype(BF16),
                             preferred_element_type=F32)
        ss = [_dot_t(k_ref[pl.ds(off, TK), sl], q_ref[:, sl]) for sl in hsl]
        if tie_order:
            over = jnp.where(eq_seen + prefix > need, 1, 0)
            kt = jnp.where(eq, kt - over, kt)
            eq_seen = eq_seen + prefix[TK - 1:TK, :]
        bias = jnp.where(kt >= tau_eff, 0.0, NEG_BIG)
        new_ms, new_ls, alphas, ps = [], [], [], []
        for h in range(C_HEADS):
            s = ss[h] + bias
            m_new = jnp.maximum(ms[h], jnp.max(s, axis=0, keepdims=True))
            alpha = jnp.exp(ms[h] - m_new)
            p = jnp.exp(s - m_new)
            new_ls.append(alpha * ls[h] + jnp.sum(p, axis=0, keepdims=True))
            new_ms.append(m_new)
            alphas.append(alpha)
            ps.append(p.astype(BF16))
        for h in range(C_HEADS):
            acc_scr[h] = alphas[h] * acc_scr[h] + jnp.dot(vt_ref[kb, hsl[h], :], ps[h],
                                                          preferred_element_type=F32)
        return eq_seen, tuple(new_ms), tuple(new_ls)

    row0 = jnp.zeros((1, Q_BLOCK), F32)
    init = (row0,
            tuple(row0 + NEG_BIG for _ in range(C_HEADS)),
            tuple(row0 for _ in range(C_HEADS)))
    cnt_ge = cnt_gt_hi + cnt_ge_lo
    _, ms, ls = lax.cond(
        jnp.max(cnt_ge) > topk,
        lambda: lax.fori_loop(0, nkb, functools.partial(p2, tie_order=True), init),
        lambda: lax.fori_loop(0, nkb, functools.partial(p2, tie_order=False), init))
    for h in range(C_HEADS):
        o_ref[:, h * LANES:(h + 1) * LANES] = (acc_scr[h] / ls[h]).T


def _dsa(q_r, qi_r, proj, k_r, v_t, ki_r, trit, B, S):
    nq = S // Q_BLOCK
    topk = min(DSA_TOPK_MAX, S // 4)
    nkt = S // DSA_TK
    assert nkt * DSA_TK == S, "sequence length must be a multiple of the key tile"
    return pl.pallas_call(
        functools.partial(_dsa_kernel, topk=topk),
        grid=(B, nq),
        in_specs=[pl.BlockSpec((Q_BLOCK, 512), lambda b, q: (b * nq + q, 0)),
                  pl.BlockSpec((Q_BLOCK, IDX_HEADS * LANES), lambda b, q: (b * nq + q, 0)),
                  pl.BlockSpec((Q_BLOCK, LANES), lambda b, q: (b * nq + q, COLBLK_IDX)),
                  pl.BlockSpec((S, 512), lambda b, q: (b, 0)),
                  pl.BlockSpec((nkt, 512, DSA_TK), lambda b, q: (b, 0, 0)),
                  pl.BlockSpec((S, LANES), lambda b, q: (b, 0)),
                  pl.BlockSpec((DSA_TK, DSA_TK), lambda b, q: (0, 0))],
        out_specs=pl.BlockSpec((Q_BLOCK, 512), lambda b, q: (b * nq + q, 0)),
        out_shape=jax.ShapeDtypeStruct((B * S, 512), F32),
        scratch_shapes=[pltpu.VMEM((IDX_HEADS * Q_BLOCK, LANES), BF16),
                        pltpu.VMEM((nkt, DSA_TK, Q_BLOCK), jnp.int32),
                        pltpu.VMEM((nkt, DSA_TK, Q_BLOCK), jnp.int16),
                        pltpu.VMEM((nkt, DSA_TK, Q_BLOCK), jnp.int16),
                        pltpu.VMEM((C_HEADS, LANES, Q_BLOCK), F32)],
        compiler_params=_cparams(("parallel", "arbitrary"), VMEM_LIMIT),
        name="dsa",
    )(q_r, qi_r, proj, k_r, v_t, ki_r, trit)


def _merge_kernel(ya_ref, yb_ref, yc_ref, g0_ref, g1_ref, g2_ref, wb_ref, wo_ref, x_ref, o_ref):
    mixed = None
    for y_ref, g_ref, n in ((ya_ref, g0_ref, 0), (yb_ref, g1_ref, 1), (yc_ref, g2_ref, 2)):
        bp = jnp.dot(y_ref[...].astype(BF16), wb_ref[n], preferred_element_type=F32)
        t = jax.nn.sigmoid(g_ref[...]) * bp
        mixed = t if mixed is None else mixed + t
    o_ref[...] = x_ref[...] + jnp.dot(mixed.astype(BF16), wo_ref[...], preferred_element_type=F32)


def _merge(y_a, y_b, y_c, proj, w_branch, w_out, x, tb=256):
    T = x.shape[0]
    tb = min(tb, T)
    yspec = pl.BlockSpec((tb, 512), lambda i: (i, 0))

    def gspec(n):
        return pl.BlockSpec((tb, D_MODEL), lambda i, n=n: (i, n))

    return pl.pallas_call(
        _merge_kernel,
        grid=(T // tb,),
        in_specs=[yspec, yspec, yspec, gspec(0), gspec(1), gspec(2),
                  pl.BlockSpec((N_BRANCH, BRANCH_WIDTH, D_MODEL), lambda i: (0, 0, 0)),
                  pl.BlockSpec((D_MODEL, D_MODEL), lambda i: (0, 0)),
                  pl.BlockSpec((tb, D_MODEL), lambda i: (i, 0))],
        out_specs=pl.BlockSpec((tb, D_MODEL), lambda i: (i, 0)),
        out_shape=jax.ShapeDtypeStruct((T, D_MODEL), F32),
        compiler_params=_cparams(("parallel",), VMEM_LIMIT),
        name="merge",
    )(y_a, y_b, y_c, proj, proj, proj, w_branch, w_out, x)


def _oddeven_merge(lo, hi, r):
    step = r * 2
    if step < hi - lo:
        yield from _oddeven_merge(lo, hi, step)
        yield from _oddeven_merge(lo + r, hi, step)
        yield from [(i, i + r) for i in range(lo + r, hi - r, step)]
    else:
        yield (lo, lo + r)


def _oddeven_sort(lo, hi):
    if hi - lo >= 1:
        mid = lo + (hi - lo) // 2
        yield from _oddeven_sort(lo, mid)
        yield from _oddeven_sort(mid + 1, hi)
        yield from _oddeven_merge(lo, hi, 1)


SUBLANES = 8
NET16 = tuple(_oddeven_sort(0, PEER_TOPK - 1))
NET8 = tuple(_oddeven_sort(0, SUBLANES - 1))
PEER_CAND = tuple((k1, k2) for k1 in range(PEER_TOPK) for k2 in range(PEER_TOPK)
                  if (k1 + 1) * (k2 + 1) <= PEER_TOPK)


def _ce(x, i, j):
    a, b = x[i], x[j]
    x[i] = jnp.maximum(a, b)
    x[j] = jnp.minimum(a, b)


def _bitonic_sort16(m):
    d = PEER_TOPK // 2
    while d >= 1:
        for i in range(PEER_TOPK):
            if (i & d) == 0:
                _ce(m, i, i + d)
        d //= 2


def _merge_sublanes(v, shifts):
    for sh in shifts:
        w = [pltpu.roll(t, sh, 0) for t in v]
        v = [jnp.maximum(v[k], w[PEER_TOPK - 1 - k]) for k in range(PEER_TOPK)]
        _bitonic_sort16(v)
    return v


def _top16_of_rows(s):
    v = [s[SUBLANES * k:SUBLANES * (k + 1), :] for k in range(PEER_N_KEYS // SUBLANES)]
    for i, j in NET16:
        _ce(v, i, j)
    return _merge_sublanes(v, (4, 2, 1))


def _peer_prep_kernel(x_ref, g_ref, wqt_ref, sk_ref, ht_ref, r2_ref, e2_ref, c_ref, cf_ref, hb_scr):
    tb = x_ref.shape[0]
    h2 = _rms(x_ref[...], g_ref[...])
    hb_scr[...] = h2.T.astype(BF16)
    ht_ref[...] = hb_scr[...]
    sub = lax.broadcasted_iota(jnp.int32, (SUBLANES, LANES), 0)
    neg = jnp.full((SUBLANES, LANES), -jnp.inf, F32)

    def head(h, carry):
        hb = hb_scr[...]
        q1 = jnp.dot(wqt_ref[pl.ds(pl.multiple_of(h * 2 * LANES, LANES), LANES), :], hb,
                     preferred_element_type=F32)
        q2 = jnp.dot(wqt_ref[pl.ds(pl.multiple_of(h * 2 * LANES + LANES, LANES), LANES), :], hb,
                     preferred_element_type=F32)
        s1f = jnp.dot(sk_ref[2 * h], q1.astype(BF16), preferred_element_type=F32)
        s2f = jnp.dot(sk_ref[2 * h + 1], q2.astype(BF16), preferred_element_type=F32)
        for lt in range(tb // LANES):
            ls = slice(lt * LANES, (lt + 1) * LANES)
            s1 = s1f[:, ls]
            s2 = s2f[:, ls]
            a = _top16_of_rows(s1)
            b = _top16_of_rows(s2)
            packed = []
            for v in range((len(PEER_CAND) + SUBLANES - 1) // SUBLANES):
                p = neg
                for s in range(SUBLANES):
                    c = SUBLANES * v + s
                    if c < len(PEER_CAND):
                        k1, k2 = PEER_CAND[c]
                        p = jnp.where(sub == s, a[k1] + b[k2], p)
                packed.append(p)
            while len(packed) < SUBLANES:
                packed.append(neg)
            for i, j in NET8:
                _ce(packed, i, j)
            w = [pltpu.roll(t, 4, 0) for t in packed]
            m = packed + w[::-1]
            _bitonic_sort16(m)
            m = _merge_sublanes(m, (2, 1))
            thr = m[PEER_TOPK - 1][0:1, :]
            zsum = m[0] - m[0] + 1.0
            for k in range(1, PEER_TOPK):
                zsum = zsum + jnp.exp(m[k] - m[0])
            zinv = 1.0 / zsum[0:1, :]
            cnt = jnp.zeros((PEER_N_KEYS, LANES), F32)
            rank = jnp.zeros((PEER_N_KEYS, LANES), F32)
            for k in range(PEER_TOPK):
                bk = b[k][0:1, :]
                cnt = cnt + jnp.where(s1 + bk >= thr, 1.0, 0.0)
                rank = rank + jnp.where(bk > s2, 1.0, 0.0)
            r2_ref[h, :, ls] = rank.astype(r2_ref.dtype)
            e2_ref[h, :, ls] = jnp.exp(s2 - b[0][0:1, :]).astype(e2_ref.dtype)
            c_ref[h, :, ls] = cnt
            cf_ref[h, :, ls] = jnp.exp(s1 - a[0][0:1, :]) * zinv
        return carry

    lax.fori_loop(0, PEER_HEADS, head, 0)


def _peer_prep(x, gain, w_qt, sub_keys, tb=256):
    T = x.shape[0]
    tb = min(tb, T)
    aux_spec = pl.BlockSpec((PEER_HEADS, PEER_N_KEYS, tb), lambda i: (0, 0, i))
    aux_f32 = jax.ShapeDtypeStruct((PEER_HEADS, PEER_N_KEYS, T), F32)
    aux_b16 = jax.ShapeDtypeStruct((PEER_HEADS, PEER_N_KEYS, T), BF16)
    return pl.pallas_call(
        _peer_prep_kernel,
        grid=(T // tb,),
        in_specs=[pl.BlockSpec((tb, D_MODEL), lambda i: (i, 0)),
                  pl.BlockSpec((1, D_MODEL), lambda i: (0, 0)),
                  pl.BlockSpec((2 * PEER_HEADS * LANES, D_MODEL), lambda i: (0, 0)),
                  pl.BlockSpec((2 * PEER_HEADS, PEER_N_KEYS, LANES), lambda i: (0, 0, 0))],
        out_specs=[pl.BlockSpec((D_MODEL, tb), lambda i: (0, i)),
                   aux_spec, aux_spec, aux_spec, aux_spec],
        out_shape=[jax.ShapeDtypeStruct((D_MODEL, T), BF16),
                   aux_b16, aux_b16, aux_f32, aux_f32],
        scratch_shapes=[pltpu.VMEM((D_MODEL, tb), BF16)],
        compiler_params=_cparams(("parallel",), VMEM_LIMIT),
        name="peer_prep",
    )(x, gain, w_qt, sub_keys)


PEER_ET = 1024
PEER_MM = 256


GELU_K1 = -2.0 * float(np.sqrt(2.0 / np.pi)) * float(np.log2(np.e))
GELU_K3 = GELU_K1 * 0.044715


def _peer_dense_kernel(x_ref, ht_ref, r2_ref, e2_ref, c_ref, cf_ref, u_ref, vt_ref, o_ref,
                       acc_scr, a_scr, ga_scr):
    et = pl.program_id(1)
    n_sub = PEER_ET // PEER_N_KEYS

    @pl.when(et == 0)
    def _():
        acc_scr[...] = jnp.zeros_like(acc_scr)

    hb = ht_ref[...]
    n_pair = PEER_ET // PEER_MM
    per_mm = PEER_MM // PEER_N_KEYS
    for pr in range(n_pair):
        ps = slice(pr * PEER_MM, (pr + 1) * PEER_MM)
        a_scr[ps, :] = jnp.dot(u_ref[ps, :], hb, preferred_element_type=F32)
    for pr in range(n_pair):
        ps = slice(pr * PEER_MM, (pr + 1) * PEER_MM)
        for half in range(per_mm):
            rs = slice(pr * PEER_MM + half * PEER_N_KEYS, pr * PEER_MM + (half + 1) * PEER_N_KEYS)
            i = et * n_sub + pr * per_mm + half
            a = a_scr[rs, :].astype(BF16)
            act = a / (1.0 + jnp.exp2(a * (GELU_K1 + GELU_K3 * (a * a))))
            g = None
            for h in range(PEER_HEADS):
                c_row = c_ref[h, pl.ds(i, 1), :].astype(r2_ref.dtype)
                cf_row = cf_ref[h, pl.ds(i, 1), :].astype(e2_ref.dtype)
                t = jnp.where(r2_ref[h] < c_row, e2_ref[h] * cf_row, jnp.zeros((), e2_ref.dtype))
                g = t if g is None else g + t
            ga_scr[rs, :] = g.astype(BF16) * act
    acc_scr[...] += jnp.dot(vt_ref[...], ga_scr[...], preferred_element_type=F32)

    @pl.when(et == pl.num_programs(1) - 1)
    def _():
        o_ref[...] = x_ref[...] + acc_scr[...].T


def _peer_dense(x, ht, r2, e2, cnt, coef, u_tab, vt_tab, tb=1024):
    T = x.shape[0]
    tb = min(tb, T)
    n_exp = u_tab.shape[0]
    once = pl.Buffered(1)
    aux_spec = pl.BlockSpec((PEER_HEADS, PEER_N_KEYS, tb), lambda i, e: (0, 0, i), pipeline_mode=once)
    return pl.pallas_call(
        _peer_dense_kernel,
        grid=(T // tb, n_exp // PEER_ET),
        in_specs=[pl.BlockSpec((tb, D_MODEL), lambda i, e: (i, 0), pipeline_mode=once),
                  pl.BlockSpec((D_MODEL, tb), lambda i, e: (0, i), pipeline_mode=once),
                  aux_spec, aux_spec, aux_spec, aux_spec,
                  pl.BlockSpec((PEER_ET, D_MODEL), lambda i, e: (e, 0)),
                  pl.BlockSpec((D_MODEL, PEER_ET), lambda i, e: (0, e))],
        out_specs=pl.BlockSpec((tb, D_MODEL), lambda i, e: (i, 0)),
        out_shape=jax.ShapeDtypeStruct((T, D_MODEL), F32),
        scratch_shapes=[pltpu.VMEM((D_MODEL, tb), F32),
                        pltpu.VMEM((PEER_ET, tb), F32),
                        pltpu.VMEM((PEER_ET, tb), BF16)],
        compiler_params=_cparams(("parallel", "arbitrary"), VMEM_LIMIT),
        name="peer_dense",
    )(x, ht, r2, e2, cnt, coef, u_tab, vt_tab)


def _final_kernel(x_ref, g_ref, o_ref):
    o_ref[...] = _rms(x_ref[...], g_ref[...])


def _final_norm(x, gain, tb=1024):
    T = x.shape[0]
    tb = min(tb, T)
    return pl.pallas_call(
        _final_kernel,
        grid=(T // tb,),
        in_specs=[pl.BlockSpec((tb, D_MODEL), lambda i: (i, 0)),
                  pl.BlockSpec((1, D_MODEL), lambda i: (0, 0))],
        out_specs=pl.BlockSpec((tb, D_MODEL), lambda i: (i, 0)),
        out_shape=jax.ShapeDtypeStruct((T, D_MODEL), F32),
        compiler_params=_cparams(("parallel",)),
        name="final_norm",
    )(x, gain)


def _rope_tables(positions):
    pos = positions.astype(F32).reshape(-1, 1)

    def tables(dim):
        half = dim // 2
        inv = ROPE_THETA ** (-jnp.arange(half, dtype=F32) / half)
        ang = pos * inv
        cos = jnp.cos(ang)
        sin = jnp.sin(ang)
        reps = LANES // dim
        return (jnp.tile(jnp.concatenate([cos, cos], axis=-1), (1, reps)),
                jnp.tile(jnp.concatenate([-sin, sin], axis=-1), (1, reps)))

    cos_c, sin_c = tables(C_HEAD_DIM)
    cos_i, sin_i = tables(IDX_DIM)
    return cos_c, sin_c, cos_i, sin_i


def _pack_w_in(w):
    n_idx = IDX_DIM + IDX_HEADS
    main = w[:, :N_MAIN]
    idx = w[:, N_MAIN:N_MAIN + n_idx]
    gates = w[:, N_MAIN + n_idx:]
    pad = jnp.zeros((w.shape[0], LANES - n_idx), w.dtype)
    return jnp.concatenate([gates, main, idx, pad], axis=1).astype(BF16)


def _layer(x, l, B, S, tabs, tri, lbp_all, p):
    proj = _inproj(x, p["norm1_gain"][l][None, :], _pack_w_in(p["w_in"][l]))
    y_a = _hgrn(proj, lbp_all[l], p["hgrn_norm_gain"][l][None, :], B, S)
    bias_full = jnp.repeat(p["gmlp_b_s"][l].T, B_BLOCK, axis=1)
    y_b = _gmlp(proj, p["gmlp_norm_gain"][l][None, :], p["gmlp_w_s"][l], bias_full)
    q_r, k_r, v_t, qi_r, ki_r = _dsa_prep(proj, *tabs)
    y_c = _dsa(q_r, qi_r, proj, k_r, v_t, ki_r, tri, B, S)
    x = _merge(y_a, y_b, y_c, proj, p["w_branch"][l].astype(BF16), p["w_out"][l].astype(BF16), x)
    sk = p["peer_sub_keys"][l].reshape(2 * PEER_HEADS, PEER_N_KEYS, LANES).astype(BF16)
    ht, r2, e2, cnt, coef = _peer_prep(x, p["norm2_gain"][l][None, :],
                                       p["peer_w_q"][l].T.astype(BF16), sk)
    return _peer_dense(x, ht, r2, e2, cnt, coef, p["peer_u"][l].astype(BF16),
                       p["peer_v"][l].T.astype(BF16))


def kernel(x, positions, norm1_gain, w_in, hgrn_lb_logits, hgrn_norm_gain, gmlp_norm_gain,
           gmlp_w_s, gmlp_b_s, w_branch, w_out, norm2_gain, peer_w_q, peer_sub_keys,
           peer_u, peer_v, final_gain):
    B, S, _ = x.shape
    depth = w_in.shape[0]
    p = dict(norm1_gain=norm1_gain, w_in=w_in, hgrn_norm_gain=hgrn_norm_gain,
             gmlp_norm_gain=gmlp_norm_gain, gmlp_w_s=gmlp_w_s, gmlp_b_s=gmlp_b_s,
             w_branch=w_branch, w_out=w_out, norm2_gain=norm2_gain, peer_w_q=peer_w_q,
             peer_sub_keys=peer_sub_keys, peer_u=peer_u, peer_v=peer_v)
    sm = jax.nn.softmax(hgrn_lb_logits.astype(F32), axis=0)
    cs = jnp.cumsum(sm, axis=0)
    lb = cs - cs[0:1]
    lbp_all = jnp.stack([jnp.log(lb), jnp.log1p(-lb), 1.0 - lb], axis=1)
    tabs = _rope_tables(positions)
    r = lax.broadcasted_iota(jnp.int32, (DSA_TK, DSA_TK), 0)
    c = lax.broadcasted_iota(jnp.int32, (DSA_TK, DSA_TK), 1)
    tri = (r >= c).astype(BF16)
    xt = x.reshape(B * S, D_MODEL)
    for l in range(depth):
        xt = _layer(xt, l, B, S, tabs, tri, lbp_all, p)
    return _final_norm(xt, final_gain[None, :]).reshape(B, S, D_MODEL)
```
